```python
import jax, jax.numpy as jnp
from jax import lax
import numpy as np

D_MODEL = 2048
BATCH = 1
SEQ = 8192
DEPTH = 1
DEC_BATCH = 32
DEC_SEQ = 4
PAST_LEN = 8192
PAGE_SIZE = 128

MIX_WIDTH = D_MODEL
SB_HEAD_DIM = 128
SB_HEADS = (MIX_WIDTH // 2) // SB_HEAD_DIM
SB_WIDTH = SB_HEADS * SB_HEAD_DIM
SB_SCALE = SB_HEAD_DIM ** -0.5
SB_BIAS_INIT = -6.0
Q_BLOCK = 128
POOL_WINDOWS = (2, 4, 8, 16)
POOL_WIDTH = MIX_WIDTH - SB_WIDTH
POOL_GROUP = POOL_WIDTH // len(POOL_WINDOWS)
POOL_STATE = max(POOL_WINDOWS) - 1
IN_WIDTH = 3 * SB_WIDTH + POOL_WIDTH
N_EXPERTS = 64
TOP_K = 8
N_EXPERT_GROUPS = 8
TOPK_GROUPS = 4
D_EXPERT = D_MODEL // 4
D_SHARED = D_MODEL // 4
ROUTED_SCALE = 2.5
D_PLE = 256
RMS_EPS = 1e-6

kernel_name = 'stickbreak_pool_hybrid_moe_decode_step'


def rmsnorm(x, g):
    xf = x.astype(jnp.float32)
    y = xf * lax.rsqrt(jnp.mean(xf * xf, axis=-1, keepdims=True) + RMS_EPS)
    return (y * g.astype(jnp.float32)).astype(x.dtype)


def project(h, g, w_in):
    b, l = h.shape[0], h.shape[1]
    z = rmsnorm(h, g) @ w_in
    heads = lambda a: a.reshape(b, l, SB_HEADS, SB_HEAD_DIM)
    q = heads(z[..., :SB_WIDTH])
    k = heads(z[..., SB_WIDTH:2 * SB_WIDTH])
    v = heads(z[..., 2 * SB_WIDTH:3 * SB_WIDTH])
    u = z[..., 3 * SB_WIDTH:]
    return q, k, v, u


def sb_weights(z, mask):
    log_keep = jnp.where(mask, jax.nn.log_sigmoid(-z), 0.0)
    after = lax.cumsum(log_keep, axis=z.ndim - 1, reverse=True)
    after = jnp.concatenate([after[..., 1:], jnp.zeros_like(after[..., :1])], axis=-1)
    return jnp.where(mask, jnp.exp(jax.nn.log_sigmoid(z) + after), 0.0)


def sb_prompt(q, k, v, sb_bias):
    b, s = q.shape[0], q.shape[1]
    nb = s // Q_BLOCK
    kpos = jnp.arange(s)
    qb = q.reshape(b, nb, Q_BLOCK, SB_HEADS, SB_HEAD_DIM).swapaxes(0, 1)
    posb = kpos.reshape(nb, Q_BLOCK)
    bias = sb_bias.astype(jnp.float32)[None, :, None, None]

    def block(args):
        qblk, qpos = args
        z = jnp.einsum('bqhd,bkhd->bhqk', qblk, k).astype(jnp.float32) * SB_SCALE + bias
        a = sb_weights(z, kpos[None, :] < qpos[:, None])
        return jnp.einsum('bhqk,bkhd->bqhd', a.astype(v.dtype), v)

    o = lax.map(block, (qb, posb))
    return o.swapaxes(0, 1).reshape(b, s, SB_WIDTH)


def sb_sample(q, k_new, v_new, k_past, v_past, sb_bias):
    db, t = q.shape[0], q.shape[1]
    p = k_past.shape[1]
    qpos = p + jnp.arange(t)
    kpos = jnp.arange(p + t)
    bias = sb_bias.astype(jnp.float32)[None, :, None, None]
    z = jnp.concatenate([jnp.einsum('bqhd,bkhd->bhqk', q, k_past),
                         jnp.einsum('bqhd,bkhd->bhqk', q, k_new)], axis=-1).astype(jnp.float32) * SB_SCALE + bias
    a = sb_weights(z, kpos[None, :] < qpos[:, None]).astype(v_new.dtype)
    o = (jnp.einsum('bhqk,bkhd->bqhd', a[..., :p], v_past)
         + jnp.einsum('bhqk,bkhd->bqhd', a[..., p:], v_new))
    return o.reshape(db, t, SB_WIDTH)


def pool_mix(u, pos, w_pool, pool_scale):
    b, l = u.shape[0], u.shape[1]
    uf = u.astype(jnp.float32)
    csum = lax.cumsum(uf, axis=1)
    parts = []
    for gi, w in enumerate(POOL_WINDOWS):
        lo = gi * POOL_GROUP
        cg = csum[:, :, lo:lo + POOL_GROUP]
        prev = jnp.pad(cg, ((0, 0), (w, 0), (0, 0)))[:, :l]
        cnt = jnp.minimum(pos + 1, w).astype(jnp.float32)[None, :, None]
        parts.append((cg - prev) / cnt - uf[:, :, lo:lo + POOL_GROUP])
    d = jnp.stack(parts, axis=2).astype(u.dtype)
    out = jnp.einsum('blgc,gce->blge', d, w_pool).reshape(b, l, POOL_WIDTH)
    return out * pool_scale.astype(out.dtype)


def moe(h, w_router, router_bias, w_gate, w_up, w_down, ws_gate, ws_up, ws_down):
    n = h.shape[0]
    s = jax.nn.sigmoid(h.astype(jnp.float32) @ w_router.astype(jnp.float32))
    sb = s + router_bias.astype(jnp.float32)
    grouped = sb.reshape(n, N_EXPERT_GROUPS, N_EXPERTS // N_EXPERT_GROUPS)
    gscore = jnp.sum(lax.top_k(grouped, 2)[0], axis=-1)
    _, gidx = lax.top_k(gscore, TOPK_GROUPS)
    gmask = jnp.sum(jax.nn.one_hot(gidx, N_EXPERT_GROUPS, dtype=jnp.float32), axis=1) > 0
    emask = jnp.repeat(gmask, N_EXPERTS // N_EXPERT_GROUPS, axis=1)
    _, eidx = lax.top_k(jnp.where(emask, sb, -jnp.inf), TOP_K)
    wsel = jnp.take_along_axis(s, eidx, axis=1)
    wsel = wsel / jnp.sum(wsel, axis=-1, keepdims=True) * ROUTED_SCALE
    combine = jnp.sum(jax.nn.one_hot(eidx, N_EXPERTS, dtype=jnp.float32) * wsel[..., None], axis=1)

    def add_expert(acc, params):
        wg, wu, wd, c = params
        hid = jax.nn.silu(h @ wg) * (h @ wu)
        return acc + (hid @ wd).astype(jnp.float32) * c[:, None], None

    routed, _ = lax.scan(add_expert, jnp.zeros((n, h.shape[1]), jnp.float32),
                         (w_gate, w_up, w_down, combine.T))
    shared = (jax.nn.silu(h @ ws_gate) * (h @ ws_up)) @ ws_down
    return (routed + shared.astype(jnp.float32)).astype(h.dtype)


def setup_inputs(seed: int = 0) -> dict:
    key = jax.random.key(seed)
    ks = jax.random.split(key, 28)
    nrm = lambda k, shape, scale: jax.random.normal(k, shape, jnp.float32) * scale
    n_pages = PAST_LEN // PAGE_SIZE
    n_used = DEC_BATCH * n_pages
    n_phys = n_used + max(1, n_used // 4)
    perm = jax.random.permutation(ks[0], n_phys)[:n_used]
    page_table = perm.reshape(DEC_BATCH, n_pages).astype(jnp.int32)
    return {
        'x_prompt': nrm(ks[1], (BATCH, SEQ, D_MODEL), 1.0),
        'x_sample': nrm(ks[2], (DEC_BATCH, DEC_SEQ, D_MODEL), 1.0),
        'cache_k': nrm(ks[3], (DEPTH, n_phys, PAGE_SIZE, SB_HEADS, SB_HEAD_DIM), 1.0),
        'cache_v': nrm(ks[4], (DEPTH, n_phys, PAGE_SIZE, SB_HEADS, SB_HEAD_DIM), 1.0),
        'state_pool': nrm(ks[5], (DEPTH, DEC_BATCH, POOL_STATE, POOL_WIDTH), 1.0),
        'page_table': page_table,
        'p_prompt': nrm(ks[6], (DEPTH, BATCH, SEQ, D_PLE), 1.0),
        'p_sample': nrm(ks[7], (DEPTH, DEC_BATCH, DEC_SEQ, D_PLE), 1.0),
        'g_mix': 1.0 + nrm(ks[8], (DEPTH, D_MODEL), 0.05),
        'w_in': nrm(ks[9], (DEPTH, D_MODEL, IN_WIDTH), D_MODEL ** -0.5),
        'sb_bias': SB_BIAS_INIT + nrm(ks[26], (DEPTH, SB_HEADS), 0.1),
        'w_pool': nrm(ks[10], (DEPTH, len(POOL_WINDOWS), POOL_GROUP, POOL_GROUP), POOL_GROUP ** -0.5),
        'pool_scale': 1.0 + nrm(ks[11], (DEPTH, POOL_WIDTH), 0.1),
        'w_o': nrm(ks[12], (DEPTH, MIX_WIDTH, D_MODEL), MIX_WIDTH ** -0.5),
        'g_ffn': 1.0 + nrm(ks[13], (DEPTH, D_MODEL), 0.05),
        'w_router': nrm(ks[14], (DEPTH, D_MODEL, N_EXPERTS), D_MODEL ** -0.5),
        'router_bias': nrm(ks[15], (DEPTH, N_EXPERTS), 0.01),
        'w_exp_gate': nrm(ks[16], (DEPTH, N_EXPERTS, D_MODEL, D_EXPERT), D_MODEL ** -0.5),
        'w_exp_up': nrm(ks[17], (DEPTH, N_EXPERTS, D_MODEL, D_EXPERT), D_MODEL ** -0.5),
        'w_exp_down': nrm(ks[18], (DEPTH, N_EXPERTS, D_EXPERT, D_MODEL), D_EXPERT ** -0.5),
        'w_sh_gate': nrm(ks[19], (DEPTH, D_MODEL, D_SHARED), D_MODEL ** -0.5),
        'w_sh_up': nrm(ks[20], (DEPTH, D_MODEL, D_SHARED), D_MODEL ** -0.5),
        'w_sh_down': nrm(ks[21], (DEPTH, D_SHARED, D_MODEL), D_SHARED ** -0.5),
        'g_ple': 1.0 + nrm(ks[22], (DEPTH, D_MODEL), 0.05),
        'w_ple_gate': nrm(ks[23], (DEPTH, D_MODEL, D_MODEL), D_MODEL ** -0.5),
        'w_ple_proj': nrm(ks[24], (DEPTH, D_PLE, D_MODEL), D_PLE ** -0.5),
        'g_final': 1.0 + nrm(ks[25], (D_MODEL,), 0.05),
    }


def reference(x_prompt, x_sample, cache_k, cache_v, state_pool, page_table, p_prompt, p_sample,
              g_mix, w_in, sb_bias, w_pool, pool_scale, w_o, g_ffn, w_router, router_bias,
              w_exp_gate, w_exp_up, w_exp_down, w_sh_gate, w_sh_up, w_sh_down,
              g_ple, w_ple_gate, w_ple_proj, g_final):
    hp, hs = x_prompt, x_sample
    db = x_sample.shape[0]
    past_len = page_table.shape[1] * cache_k.shape[2]
    pos_p = jnp.arange(hp.shape[1])
    kp_l, vp_l, up_l, ks_l, vs_l, us_l = [], [], [], [], [], []
    for i in range(DEPTH):
        qp, kp, vp, up = project(hp, g_mix[i], w_in[i])
        qs, ks, vs, us = project(hs, g_mix[i], w_in[i])
        att_p = sb_prompt(qp, kp, vp, sb_bias[i])
        k_past = cache_k[i][page_table].reshape(db, past_len, SB_HEADS, SB_HEAD_DIM)
        v_past = cache_v[i][page_table].reshape(db, past_len, SB_HEADS, SB_HEAD_DIM)
        att_s = sb_sample(qs, ks, vs, k_past, v_past, sb_bias[i])
        pool_p = pool_mix(up, pos_p, w_pool[i], pool_scale[i])
        buf = jnp.concatenate([state_pool[i].astype(us.dtype), us], axis=1)
        pos_buf = past_len - POOL_STATE + jnp.arange(buf.shape[1])
        pool_s = pool_mix(buf, pos_buf, w_pool[i], pool_scale[i])[:, POOL_STATE:]
        hp = hp + jnp.concatenate([att_p, pool_p], axis=-1) @ w_o[i]
        hs = hs + jnp.concatenate([att_s, pool_s], axis=-1) @ w_o[i]
        kp_l.append(kp); vp_l.append(vp); up_l.append(up[:, -POOL_STATE:])
        ks_l.append(ks); vs_l.append(vs); us_l.append(buf[:, -POOL_STATE:])
        n_p = hp.shape[0] * hp.shape[1]
        t = jnp.concatenate([hp.reshape(-1, D_MODEL), hs.reshape(-1, D_MODEL)], axis=0)
        t = t + moe(rmsnorm(t, g_ffn[i]), w_router[i], router_bias[i], w_exp_gate[i], w_exp_up[i],
                    w_exp_down[i], w_sh_gate[i], w_sh_up[i], w_sh_down[i])
        pe = jnp.concatenate([p_prompt[i].reshape(-1, D_PLE), p_sample[i].reshape(-1, D_PLE)],
                             axis=0).astype(t.dtype) @ w_ple_proj[i]
        t = t + pe * jax.nn.sigmoid(rmsnorm(t, g_ple[i]) @ w_ple_gate[i])
        hp = t[:n_p].reshape(hp.shape)
        hs = t[n_p:].reshape(hs.shape)
    y_prompt = rmsnorm(hp, g_final)
    y_sample = rmsnorm(hs, g_final)
    new_k_prompt = jnp.stack(kp_l)
    new_v_prompt = jnp.stack(vp_l)
    new_pool_prompt = jnp.stack(up_l)
    new_k_sample = jnp.stack(ks_l)
    new_v_sample = jnp.stack(vs_l)
    new_pool_sample = jnp.stack(us_l)
    return (y_prompt, y_sample, new_k_prompt, new_v_prompt, new_pool_prompt, new_k_sample, new_v_sample, new_pool_sample)
```

```python
import functools

import jax
import jax.numpy as jnp
from jax import lax
from jax.experimental import pallas as pl
from jax.experimental.pallas import tpu as pltpu

F32 = jnp.float32
BF16 = jnp.bfloat16

RMS_EPS = 1e-6
POOL_WINDOWS = (2, 4, 8, 16)
POOL_STATE = max(POOL_WINDOWS) - 1
TOP_K = 8
N_EXPERT_GROUPS = 8
TOPK_GROUPS = 4
ROUTED_SCALE = 2.5

V7X_VMEM_LIMIT_BYTES = 56 * 1024 * 1024


def _params(*sem):
    return pltpu.CompilerParams(dimension_semantics=sem, vmem_limit_bytes=V7X_VMEM_LIMIT_BYTES)


def _rms(x, g):
    return x * lax.rsqrt(jnp.mean(x * x, axis=-1, keepdims=True) + RMS_EPS) * g


def _split3(x):
    hi = x.astype(BF16)
    r = x - hi.astype(F32)
    mid = r.astype(BF16)
    lo = (r - mid.astype(F32)).astype(BF16)
    return hi, mid, lo


def _dot(a, b):
    return jnp.dot(a, b, preferred_element_type=F32)


def _dot_nt(a, b):
    return lax.dot_general(a, b, (((1,), (1,)), ((), ())), preferred_element_type=F32)


def _inproj_kernel(x_ref, g_ref, w_ref, q_ref, k_ref, v_ref, u_ref, kb_ref, vb_ref, xn_ref):
    j = pl.program_id(1)

    @pl.when(j == 0)
    def _():
        xn_ref[...] = _rms(x_ref[...], g_ref[...]).astype(BF16)

    z = _dot(xn_ref[...], w_ref[...])

    @pl.when(j == 0)
    def _():
        q_ref[...] = z.astype(BF16)

    @pl.when(j == 1)
    def _():
        k_ref[...] = z
        kb_ref[...] = z.astype(BF16)

    @pl.when(j == 2)
    def _():
        v_ref[...] = z
        vb_ref[...] = z.astype(BF16)

    @pl.when(j == 3)
    def _():
        u_ref[...] = z


def _inproj(x, g, w_bf16, tm):
    n, d = x.shape
    wn = w_bf16.shape[1] // 4
    out = lambda dt: jax.ShapeDtypeStruct((n, wn), dt)
    ospec = pl.BlockSpec((tm, wn), lambda i, j: (i, 0))
    return pl.pallas_call(
        _inproj_kernel,
        out_shape=(out(BF16), out(F32), out(F32), out(F32), out(BF16), out(BF16)),
        grid=(n // tm, 4),
        in_specs=[pl.BlockSpec((tm, d), lambda i, j: (i, 0)),
                  pl.BlockSpec((1, d), lambda i, j: (0, 0)),
                  pl.BlockSpec((d, wn), lambda i, j: (0, j))],
        out_specs=(ospec,) * 6,
        scratch_shapes=[pltpu.VMEM((tm, d), BF16)],
        compiler_params=_params("parallel", "arbitrary"),
        name="inproj",
    )(x, g, w_bf16)


def _sb_tile(s, scale, bias, upper, c, mask):
    z = s * scale + bias
    sp = jnp.maximum(z, 0.0) + jnp.log(1.0 + jnp.exp(-jnp.abs(z)))
    lk = -sp
    if mask is not None:
        lk = jnp.where(mask, lk, 0.0)
    hi = lk.astype(BF16)
    lo = (lk - hi.astype(F32)).astype(BF16)
    after = _dot(hi, upper) + _dot(lo, upper)
    a = jnp.exp((z - sp) + after + c)
    if mask is not None:
        a = jnp.where(mask, a, 0.0)
    return a, c + jnp.sum(lk, axis=-1, keepdims=True)


def _strict_upper(t):
    return (lax.broadcasted_iota(jnp.int32, (t, t), 0) > lax.broadcasted_iota(jnp.int32, (t, t), 1)).astype(BF16)


def _sb_prompt_kernel(bias_ref, q_ref, k_ref, v_ref, o_ref, *, tq, scale):
    h = pl.program_id(0)
    i = pl.program_id(1)
    bias = bias_ref[h]
    q = q_ref[...]
    upper = _strict_upper(tq)
    causal = lax.broadcasted_iota(jnp.int32, (tq, tq), 1) < lax.broadcasted_iota(jnp.int32, (tq, tq), 0)

    def block(j, c, acc, mask):
        rows = pl.ds(pl.multiple_of(j * tq, tq), tq)
        a, c = _sb_tile(_dot_nt(q, k_ref[rows, :]), scale, bias, upper, c, mask)
        return c, acc + _dot(a.astype(BF16), v_ref[rows, :])

    c, acc = block(i, jnp.zeros((tq, 1), F32), jnp.zeros(o_ref.shape, F32), causal)

    def body(jj, carry):
        return block(i - 1 - jj, carry[0], carry[1], None)

    c, acc = lax.fori_loop(0, i, body, (c, acc))
    o_ref[...] = acc.astype(o_ref.dtype)


def _sb_prompt(q, kb, vb, sb_bias, head_dim, tq):
    s, width = q.shape
    heads = width // head_dim
    kv_spec = pl.BlockSpec((s, head_dim), lambda h, i, b: (0, h))
    return pl.pallas_call(
        functools.partial(_sb_prompt_kernel, tq=tq, scale=head_dim ** -0.5),
        out_shape=jax.ShapeDtypeStruct((s, width), BF16),
        grid_spec=pltpu.PrefetchScalarGridSpec(
            num_scalar_prefetch=1,
            grid=(heads, s // tq),
            in_specs=[pl.BlockSpec((tq, head_dim), lambda h, i, b: (i, h)), kv_spec, kv_spec],
            out_specs=pl.BlockSpec((tq, head_dim), lambda h, i, b: (i, h)),
        ),
        compiler_params=_params("parallel", "arbitrary"),
        name="sb_prompt",
    )(sb_bias, q, kb, vb)


def _sb_sample_kernel(pt_ref, qbd_ref, bias_ref, kn_ref, vn_ref, kp_ref, vp_ref, o_ref, c_ref, acc_ref,
                      *, heads, head_dim, scale):
    p = pl.program_id(1)
    rows, page = qbd_ref.shape[1], kp_ref.shape[1]
    qbd = qbd_ref[0]
    bias = bias_ref[...]
    upper = _strict_upper(page)

    @pl.when(p == 0)
    def _():
        t_of_row = lax.broadcasted_iota(jnp.int32, (rows, page), 0) // heads
        mask = lax.broadcasted_iota(jnp.int32, (rows, page), 1) < t_of_row
        a, c = _sb_tile(_dot_nt(qbd, kn_ref[0]), scale, bias, upper, jnp.zeros((rows, 1), F32), mask)
        c_ref[...] = c
        acc_ref[...] = _dot(a.astype(BF16), vn_ref[0])

    a, c = _sb_tile(_dot_nt(qbd, kp_ref[0].astype(BF16)), scale, bias, upper, c_ref[...], None)
    c_ref[...] = c
    acc_ref[...] += _dot(a.astype(BF16), vp_ref[0].astype(BF16))

    @pl.when(p == pl.num_programs(1) - 1)
    def _():
        width = heads * head_dim
        lane_head = lax.broadcasted_iota(jnp.int32, (heads, width), 1) // head_dim
        own = lane_head == lax.broadcasted_iota(jnp.int32, (heads, width), 0)
        acc = acc_ref[...].reshape(rows // heads, heads, width)
        o_ref[0] = jnp.sum(jnp.where(own[None], acc, 0.0), axis=1)


def _sb_sample(q, k_new, v_new, cache_k, cache_v, page_table, sb_bias, heads, head_dim):
    db, t, width = q.shape
    n_pages = page_table.shape[1]
    page = cache_k.shape[1]
    rows = t * heads
    lane_head = jnp.arange(width) // head_dim
    own = (lane_head[None, :] == jnp.arange(heads)[:, None])
    qbd = jnp.where(own[None, None], q[:, :, None, :], jnp.zeros((), q.dtype)).reshape(db, rows, width)
    bias_rows = jnp.tile(sb_bias.astype(F32), t).reshape(rows, 1)
    pad = lambda a: jnp.pad(a, ((0, 0), (0, page - t), (0, 0)))
    new_spec = pl.BlockSpec((1, page, width), lambda b, p, pt: (b, 0, 0))
    page_spec = pl.BlockSpec((1, page, width), lambda b, p, pt: (pt[b, n_pages - 1 - p], 0, 0))
    return pl.pallas_call(
        functools.partial(_sb_sample_kernel, heads=heads, head_dim=head_dim, scale=head_dim ** -0.5),
        out_shape=jax.ShapeDtypeStruct((db, t, width), F32),
        grid_spec=pltpu.PrefetchScalarGridSpec(
            num_scalar_prefetch=1,
            grid=(db, n_pages),
            in_specs=[pl.BlockSpec((1, rows, width), lambda b, p, pt: (b, 0, 0)),
                      pl.BlockSpec((rows, 1), lambda b, p, pt: (0, 0)),
                      new_spec, new_spec, page_spec, page_spec],
            out_specs=pl.BlockSpec((1, t, width), lambda b, p, pt: (b, 0, 0)),
            scratch_shapes=[pltpu.VMEM((rows, 1), F32), pltpu.VMEM((rows, width), F32)],
        ),
        compiler_params=_params("parallel", "arbitrary"),
        name="sb_sample",
    )(page_table, qbd, bias_rows, pad(k_new), pad(v_new), cache_k, cache_v)


def _pool_groups(ext, tok, band_fn, inv_cnt_fn, wp_ref, scale_ref, o_ref):
    group = wp_ref.shape[1]
    for gi, w in enumerate(POOL_WINDOWS):
        cols = slice(gi * group, (gi + 1) * group)
        band = band_fn(w)
        hi, mid, lo = _split3(ext[:, cols])
        wsum = _dot(band, hi) + _dot(band, mid) + _dot(band, lo)
        d = wsum * inv_cnt_fn(w) - tok[:, cols]
        o_ref[:, cols] = (_dot(d.astype(BF16), wp_ref[gi]) * scale_ref[:, cols]).astype(o_ref.dtype)


def _pool_prompt_kernel(u_ref, prev_ref, wp_ref, scale_ref, o_ref, *, tm, halo):
    i = pl.program_id(0)
    tok = u_ref[...]
    prev = jnp.where(i > 0, prev_ref[...], 0.0)
    ext = jnp.concatenate([prev, tok], axis=0)
    r = lax.broadcasted_iota(jnp.int32, (tm, tm + halo), 0)
    cc = lax.broadcasted_iota(jnp.int32, (tm, tm + halo), 1) - halo
    pos = i * tm + lax.broadcasted_iota(jnp.int32, (tm, 1), 0)

    band_fn = lambda w: ((cc <= r) & (cc > r - w)).astype(BF16)
    inv_cnt_fn = lambda w: 1.0 / jnp.minimum(pos + 1, w).astype(F32)
    _pool_groups(ext, tok, band_fn, inv_cnt_fn, wp_ref, scale_ref, o_ref)


def _pool_prompt(u, w_pool_bf16, pool_scale, tm):
    s, width = u.shape
    halo = 128
    assert halo > POOL_STATE and tm % halo == 0
    return pl.pallas_call(
        functools.partial(_pool_prompt_kernel, tm=tm, halo=halo),
        out_shape=jax.ShapeDtypeStruct((s, width), BF16),
        grid=(s // tm,),
        in_specs=[pl.BlockSpec((tm, width), lambda i: (i, 0)),
                  pl.BlockSpec((halo, width), lambda i: (jnp.maximum(i * (tm // halo) - 1, 0), 0)),
                  pl.BlockSpec(w_pool_bf16.shape, lambda i: (0, 0, 0)),
                  pl.BlockSpec((1, width), lambda i: (0, 0))],
        out_specs=pl.BlockSpec((tm, width), lambda i: (i, 0)),
        compiler_params=_params("parallel"),
        name="pool_prompt",
    )(u, u, w_pool_bf16, pool_scale)


def _pool_sample_kernel(buf_ref, wp_ref, scale_ref, o_ref, *, t_new, buf_len, last_pos):
    ext = buf_ref[...]
    n_out = o_ref.shape[0]
    r = lax.broadcasted_iota(jnp.int32, (n_out, ext.shape[0]), 0)
    cc = lax.broadcasted_iota(jnp.int32, (n_out, ext.shape[0]), 1)
    own_row = (r // t_new) * buf_len + (buf_len - t_new) + r % t_new
    pick = (cc == own_row).astype(BF16)
    tok = sum(_dot(pick, part) for part in _split3(ext))
    pos = last_pos - (t_new - 1) + lax.broadcasted_iota(jnp.int32, (n_out, 1), 0) % t_new

    band_fn = lambda w: ((cc <= own_row) & (cc > own_row - w)).astype(BF16)
    inv_cnt_fn = lambda w: 1.0 / jnp.minimum(pos + 1, w).astype(F32)
    _pool_groups(ext, tok, band_fn, inv_cnt_fn, wp_ref, scale_ref, o_ref)


def _pool_sample(buf, w_pool_bf16, pool_scale, t_new, last_pos):
    db, buf_len, width = buf.shape
    lead = max(POOL_WINDOWS) - (buf_len - t_new) - 1
    assert lead >= 0
    while (db * (buf_len + lead)) % 128:
        lead += 1
    padded = jnp.pad(buf, ((0, 0), (lead, 0), (0, 0)))
    full = lambda shape: pl.BlockSpec(shape, lambda i: (0,) * len(shape))
    return pl.pallas_call(
        functools.partial(_pool_sample_kernel, t_new=t_new, buf_len=buf_len + lead, last_pos=last_pos),
        out_shape=jax.ShapeDtypeStruct((db * t_new, width), BF16),
        grid=(1,),
        in_specs=[full((db * (buf_len + lead), width)), full(w_pool_bf16.shape), full((1, width))],
        out_specs=full((db * t_new, width)),
        compiler_params=_params("arbitrary"),
        name="pool_sample",
    )(padded.reshape(db * (buf_len + lead), width), w_pool_bf16, pool_scale)


def _outproj_kernel(x_ref, att_ref, pool_ref, wa_ref, wp_ref, o_ref):
    o_ref[...] = x_ref[...] + _dot(att_ref[...], wa_ref[...]) + _dot(pool_ref[...], wp_ref[...])


def _outproj(x, att, pool, w_o_bf16, tm):
    n, d = x.shape
    half = att.shape[1]
    return pl.pallas_call(
        _outproj_kernel,
        out_shape=jax.ShapeDtypeStruct((n, d), F32),
        grid=(n // tm,),
        in_specs=[pl.BlockSpec((tm, d), lambda i: (i, 0)),
                  pl.BlockSpec((tm, half), lambda i: (i, 0)),
                  pl.BlockSpec((tm, half), lambda i: (i, 0)),
                  pl.BlockSpec((half, d), lambda i: (0, 0)),
                  pl.BlockSpec((half, d), lambda i: (1, 0))],
        out_specs=pl.BlockSpec((tm, d), lambda i: (i, 0)),
        compiler_params=_params("parallel"),
        name="outproj",
    )(x, att, pool, w_o_bf16, w_o_bf16)


def _rank_lt(x, k):
    nrows = x.shape[0]
    ridx = lax.broadcasted_iota(jnp.int32, x.shape, 0)
    rank = jnp.zeros(x.shape, jnp.int32)
    for e in range(nrows):
        row = x[e:e + 1, :]
        ahead = (row > x) | ((row == x) & (e < ridx))
        rank = rank + ahead.astype(jnp.int32)
    return rank < k


def _router_kernel(t_ref, g_ref, wr_ref, rb_ref, hn_ref, comb_ref):
    hn = _rms(t_ref[...], g_ref[...])
    hn_ref[...] = hn.astype(BF16)
    h_hi, h_mid, _ = _split3(hn)
    w_hi, w_mid, _ = _split3(wr_ref[...])
    logits = _dot_nt(w_hi, h_hi) + _dot_nt(w_hi, h_mid) + _dot_nt(w_mid, h_hi)
    s = jax.nn.sigmoid(logits)
    sb = s + rb_ref[...]
    n_exp, tm = sb.shape
    per_group = n_exp // N_EXPERT_GROUPS
    g3 = sb.reshape(N_EXPERT_GROUPS, per_group, tm)
    m1 = jnp.max(g3, axis=1, keepdims=True)
    is_top = g3 == m1
    n_top = jnp.sum(is_top.astype(jnp.int32), axis=1, keepdims=True)
    m2 = jnp.where(n_top > 1, m1, jnp.max(jnp.where(is_top, -jnp.inf, g3), axis=1, keepdims=True))
    gscore = (m1 + m2).reshape(N_EXPERT_GROUPS, tm)
    gsel = _rank_lt(gscore, TOPK_GROUPS)
    emask = jnp.broadcast_to(gsel[:, None, :], g3.shape).reshape(n_exp, tm)
    sel = _rank_lt(jnp.where(emask, sb, -jnp.inf), TOP_K) & emask
    wsel = jnp.where(sel, s, 0.0)
    comb_ref[...] = wsel / jnp.sum(wsel, axis=0, keepdims=True) * ROUTED_SCALE


def _router(t, g, w_router_t, router_bias, tm):
    n, d = t.shape
    n_exp = w_router_t.shape[0]
    return pl.pallas_call(
        _router_kernel,
        out_shape=(jax.ShapeDtypeStruct((n, d), BF16), jax.ShapeDtypeStruct((n_exp, n), F32)),
        grid=(n // tm,),
        in_specs=[pl.BlockSpec((tm, d), lambda i: (i, 0)),
                  pl.BlockSpec((1, d), lambda i: (0, 0)),
                  pl.BlockSpec((n_exp, d), lambda i: (0, 0)),
                  pl.BlockSpec((n_exp, 1), lambda i: (0, 0))],
        out_specs=(pl.BlockSpec((tm, d), lambda i: (i, 0)), pl.BlockSpec((n_exp, tm), lambda i: (0, i))),
        compiler_params=_params("parallel"),
        name="router",
    )(t, g, w_router_t, router_bias)


def _swiglu(h, wg, wu):
    gate = _dot(h, wg)
    return gate * jax.nn.sigmoid(gate) * _dot(h, wu)


def _moe_kernel(h_ref, comb_ref, wg_ref, wu_ref, wd_ref, sg_ref, su_ref, sd_ref, o_ref):
    e = pl.program_id(1)
    h = h_ref[...]

    @pl.when(e == 0)
    def _():
        o_ref[...] = _dot(_swiglu(h, sg_ref[...], su_ref[...]).astype(BF16), sd_ref[...])

    comb = comb_ref[...]
    pick = (lax.broadcasted_iota(jnp.int32, (comb.shape[1], 128), 0) == e).astype(BF16)
    c = sum(_dot(part, pick) for part in _split3(comb))
    hid = _swiglu(h, wg_ref[0], wu_ref[0])
    hid = hid * jnp.tile(c, (1, hid.shape[1] // 128))
    o_ref[...] += _dot(hid.astype(BF16), wd_ref[0])


def _moe(hn, comb, wg, wu, wd, sg, su, sd, tm):
    n, d = hn.shape
    n_exp, _, de = wg.shape
    ds = sg.shape[1]
    return pl.pallas_call(
        _moe_kernel,
        out_shape=jax.ShapeDtypeStruct((n, d), F32),
        grid=(n // tm, n_exp),
        in_specs=[pl.BlockSpec((tm, d), lambda i, e: (i, 0)),
                  pl.BlockSpec((tm, comb.shape[1]), lambda i, e: (i, 0)),
                  pl.BlockSpec((1, d, de), lambda i, e: (e, 0, 0)),
                  pl.BlockSpec((1, d, de), lambda i, e: (e, 0, 0)),
                  pl.BlockSpec((1, de, d), lambda i, e: (e, 0, 0)),
                  pl.BlockSpec((d, ds), lambda i, e: (0, 0)),
                  pl.BlockSpec((d, ds), lambda i, e: (0, 0)),
                  pl.BlockSpec((ds, d), lambda i, e: (0, 0))],
        out_specs=pl.BlockSpec((tm, d), lambda i, e: (i, 0)),
        compiler_params=_params("parallel", "arbitrary"),
        name="moe_dense",
    )(hn, comb, wg, wu, wd, sg, su, sd)


def _ple_kernel(t_ref, m_ref, p_ref, gp_ref, wg_ref, wp_ref, gf_ref, o_ref):
    t = t_ref[...] + m_ref[...]
    gate = jax.nn.sigmoid(_dot(_rms(t, gp_ref[...]).astype(BF16), wg_ref[...]))
    pe = _dot(p_ref[...].astype(BF16), wp_ref[...])
    o_ref[...] = _rms(t + pe * gate, gf_ref[...])


def _ple(t, m, p, g_ple, w_gate_bf16, w_proj_bf16, g_final, tm):
    n, d = t.shape
    dp = p.shape[1]
    tile = lambda w: pl.BlockSpec((tm, w), lambda i: (i, 0))
    full = lambda a, b: pl.BlockSpec((a, b), lambda i: (0, 0))
    return pl.pallas_call(
        _ple_kernel,
        out_shape=jax.ShapeDtypeStruct((n, d), F32),
        grid=(n // tm,),
        in_specs=[tile(d), tile(d), tile(dp), full(1, d), full(d, d), full(dp, d), full(1, d)],
        out_specs=tile(d),
        compiler_params=_params("parallel"),
        name="ple_final",
    )(t, m, p, g_ple, w_gate_bf16, w_proj_bf16, g_final)


def _tile(n, candidates):
    for c in candidates:
        if n % c == 0:
            return c
    raise ValueError(f"no tile for {n}")


def kernel(x_prompt, x_sample, cache_k, cache_v, state_pool, page_table, p_prompt, p_sample, g_mix, w_in, sb_bias, w_pool, pool_scale, w_o, g_ffn, w_router, router_bias, w_exp_gate, w_exp_up, w_exp_down, w_sh_gate, w_sh_up, w_sh_down, g_ple, w_ple_gate, w_ple_proj, g_final):
    depth = w_in.shape[0]
    assert depth == 1 and x_prompt.shape[0] == 1
    _, seq, d = x_prompt.shape
    db, t_new, _ = x_sample.shape
    _, n_phys, page, heads, head_dim = cache_k.shape
    sbw = heads * head_dim
    past_len = page_table.shape[1] * page
    row = lambda a: a.reshape(1, -1).astype(F32)

    xp = x_prompt.reshape(seq, d)
    xs = x_sample.reshape(db * t_new, d)
    w_in_b = w_in[0].astype(BF16)

    qp, kp, vp, up, kpb, vpb = _inproj(xp, row(g_mix[0]), w_in_b, _tile(seq, (512, 256, 128)))
    qs, ks, vs, us, ksb, vsb = _inproj(xs, row(g_mix[0]), w_in_b, db * t_new)

    att_p = _sb_prompt(qp, kpb, vpb, sb_bias[0].astype(F32), head_dim, 256)
    att_s = _sb_sample(qs.reshape(db, t_new, sbw), ksb.reshape(db, t_new, sbw), vsb.reshape(db, t_new, sbw),
                       cache_k.reshape(n_phys, page, sbw), cache_v.reshape(n_phys, page, sbw),
                       page_table, sb_bias[0], heads, head_dim)

    w_pool_b = w_pool[0].astype(BF16)
    pool_p = _pool_prompt(up, w_pool_b, row(pool_scale[0]), 256)
    buf = jnp.concatenate([state_pool[0], us.reshape(db, t_new, -1)], axis=1)
    pool_s = _pool_sample(buf, w_pool_b, row(pool_scale[0]), t_new, past_len + t_new - 1)

    x_all = jnp.concatenate([xp, xs], axis=0)
    att_all = jnp.concatenate([att_p, att_s.reshape(db * t_new, sbw).astype(BF16)], axis=0)
    pool_all = jnp.concatenate([pool_p, pool_s], axis=0)
    n = x_all.shape[0]
    tm = _tile(n, (640, 512, 256, 128))
    t1 = _outproj(x_all, att_all, pool_all, w_o[0].astype(BF16), tm)

    hn, comb_t = _router(t1, row(g_ffn[0]), w_router[0].T, router_bias[0].reshape(-1, 1).astype(F32), tm)
    comb = jnp.pad(comb_t.T, ((0, 0), (0, -comb_t.shape[0] % 128)))
    routed = _moe(hn, comb, w_exp_gate[0].astype(BF16), w_exp_up[0].astype(BF16), w_exp_down[0].astype(BF16),
                  w_sh_gate[0].astype(BF16), w_sh_up[0].astype(BF16), w_sh_down[0].astype(BF16),
                  _tile(n, (832, 640, 512, 256, 128)))

    p_all = jnp.concatenate([p_prompt[0].reshape(seq, -1), p_sample[0].reshape(db * t_new, -1)], axis=0)
    y = _ple(t1, routed, p_all, row(g_ple[0]), w_ple_gate[0].astype(BF16), w_ple_proj[0].astype(BF16),
             row(g_final), _tile(n, (320, 256, 128)))

    heads5 = lambda a, b, l: a.reshape(1, b, l, heads, head_dim)
    return (y[:seq].reshape(x_prompt.shape), y[seq:].reshape(x_sample.shape),
            heads5(kp, 1, seq), heads5(vp, 1, seq), up[None, None, seq - POOL_STATE:, :],
            heads5(ks, db, t_new), heads5(vs, db, t_new), buf[None, :, -POOL_STATE:, :])
```

```python
import functools

import jax
import jax.numpy as jnp
from jax import lax
from jax.experimental import pallas as pl
from jax.experimental.pallas import tpu as pltpu

F32 = jnp.float32
BF16 = jnp.bfloat16

RMS_EPS = 1e-6
POOL_WINDOWS = (2, 4, 8, 16)
POOL_STATE = max(POOL_WINDOWS) - 1
TOP_K = 8
N_EXPERT_GROUPS = 8
TOPK_GROUPS = 4
ROUTED_SCALE = 2.5

V7X_VMEM_LIMIT_BYTES = 56 * 1024 * 1024


def _params(*sem):
    return pltpu.CompilerParams(dimension_semantics=sem, vmem_limit_bytes=V7X_VMEM_LIMIT_BYTES)


def _rms(x, g):
    return x * lax.rsqrt(jnp.mean(x * x, axis=-1, keepdims=True) + RMS_EPS) * g


def _split3(x):
    hi = x.astype(BF16)
    r = x - hi.astype(F32)
    mid = r.astype(BF16)
    lo = (r - mid.astype(F32)).astype(BF16)
    return hi, mid, lo


def _dot(a, b):
    return jnp.dot(a, b, preferred_element_type=F32)


def _dot_nt(a, b):
    return lax.dot_general(a, b, (((1,), (1,)), ((), ())), preferred_element_type=F32)


def _inproj_kernel(x_ref, g_ref, w_ref, q_ref, k_ref, v_ref, u_ref, kb_ref, vb_ref, xn_ref):
    j = pl.program_id(1)

    @pl.when(j == 0)
    def _():
        xn_ref[...] = _rms(x_ref[...], g_ref[...]).astype(BF16)

    z = _dot(xn_ref[...], w_ref[...])

    @pl.when(j == 0)
    def _():
        q_ref[...] = z.astype(BF16)

    @pl.when(j == 1)
    def _():
        k_ref[...] = z
        kb_ref[...] = z.astype(BF16)

    @pl.when(j == 2)
    def _():
        v_ref[...] = z
        vb_ref[...] = z.astype(BF16)

    @pl.when(j == 3)
    def _():
        u_ref[...] = z


def _inproj(x, g, w_bf16, tm):
    n, d = x.shape
    wn = w_bf16.shape[1] // 4
    out = lambda dt: jax.ShapeDtypeStruct((n, wn), dt)
    ospec = pl.BlockSpec((tm, wn), lambda i, j: (i, 0))
    return pl.pallas_call(
        _inproj_kernel,
        out_shape=(out(BF16), out(F32), out(F32), out(F32), out(BF16), out(BF16)),
        grid=(n // tm, 4),
        in_specs=[pl.BlockSpec((tm, d), lambda i, j: (i, 0)),
                  pl.BlockSpec((1, d), lambda i, j: (0, 0)),
                  pl.BlockSpec((d, wn), lambda i, j: (0, j))],
        out_specs=(ospec,) * 6,
        scratch_shapes=[pltpu.VMEM((tm, d), BF16)],
        compiler_params=_params("parallel", "arbitrary"),
        name="inproj",
    )(x, g, w_bf16)


def _sb_logits(s, scale, bias, mask):
    z = s * scale + bias
    sp = jnp.maximum(z, 0.0) + jnp.log(1.0 + jnp.exp(-jnp.abs(z)))
    lk = -sp
    if mask is not None:
        lk = jnp.where(mask, lk, 0.0)
    return z - sp, lk


def _sb_after(lk, upper):
    hi = lk.astype(BF16)
    lo = (lk - hi.astype(F32)).astype(BF16)
    return _dot(hi, upper) + _dot(lo, upper)


def _sb_weights(ls, lk, after, c, mask):
    a = jnp.exp(ls + after + c)
    if mask is not None:
        a = jnp.where(mask, a, 0.0)
    return a, c + jnp.sum(lk, axis=-1, keepdims=True)


def _sb_tile(s, scale, bias, upper, c, mask):
    ls, lk = _sb_logits(s, scale, bias, mask)
    return _sb_weights(ls, lk, _sb_after(lk, upper), c, mask)


def _strict_upper(t):
    return (lax.broadcasted_iota(jnp.int32, (t, t), 0) > lax.broadcasted_iota(jnp.int32, (t, t), 1)).astype(BF16)


def _sb_prompt_kernel(bias_ref, q_ref, k_ref, v_ref, o_ref, *, tq, tk, head_dim, scale):
    hg = pl.program_id(0)
    i = pl.program_id(1)
    n_heads = q_ref.shape[1] // head_dim
    lanes = [slice(h * head_dim, (h + 1) * head_dim) for h in range(n_heads)]
    bias = [bias_ref[hg * n_heads + h] for h in range(n_heads)]
    q = [q_ref[:, l] for l in lanes]
    upper = _strict_upper(tk)
    last = (i * tq) // tk
    causal = (lax.broadcasted_iota(jnp.int32, (tq, tk), 1)
              < lax.broadcasted_iota(jnp.int32, (tq, tk), 0) + (i * tq - last * tk))

    def block(j, carry, mask):
        rows = pl.ds(pl.multiple_of(j * tk, tk), tk)
        hs = range(n_heads)
        s = [_dot_nt(q[h], k_ref[rows, lanes[h]]) for h in hs]
        lg = [_sb_logits(s[h], scale, bias[h], mask) for h in hs]
        after = [_sb_after(lg[h][1], upper) for h in hs]
        w = [_sb_weights(lg[h][0], lg[h][1], after[h], carry[h][0], mask) for h in hs]
        return tuple((w[h][1], carry[h][1] + _dot(w[h][0].astype(BF16), v_ref[rows, lanes[h]])) for h in hs)

    zero = (jnp.zeros((tq, 1), F32), jnp.zeros((tq, head_dim), F32))
    carry = block(last, (zero,) * n_heads, causal)
    carry = lax.fori_loop(0, last, lambda jj, carry: block(last - 1 - jj, carry, None), carry)
    for h in range(n_heads):
        o_ref[:, lanes[h]] = carry[h][1].astype(o_ref.dtype)


def _sb_prompt(q, kb, vb, sb_bias, head_dim, tq, tk, heads_per_step):
    s, width = q.shape
    assert tk % tq == 0 and s % tk == 0
    gw = heads_per_step * head_dim
    kv_spec = pl.BlockSpec((s, gw), lambda h, i, b: (0, h))
    return pl.pallas_call(
        functools.partial(_sb_prompt_kernel, tq=tq, tk=tk, head_dim=head_dim, scale=head_dim ** -0.5),
        out_shape=jax.ShapeDtypeStruct((s, width), BF16),
        grid_spec=pltpu.PrefetchScalarGridSpec(
            num_scalar_prefetch=1,
            grid=(width // gw, s // tq),
            in_specs=[pl.BlockSpec((tq, gw), lambda h, i, b: (i, h)), kv_spec, kv_spec],
            out_specs=pl.BlockSpec((tq, gw), lambda h, i, b: (i, h)),
        ),
        compiler_params=_params("parallel", "arbitrary"),
        name="sb_prompt",
    )(sb_bias, q, kb, vb)


def _page_matrix(page_ref, heads):
    keys = page_ref.shape[1] // heads
    cols = [page_ref[0, pl.ds(h, keys, stride=heads), :] for h in range(heads)]
    return jnp.concatenate(cols, axis=1).astype(BF16)


def _sb_sample_kernel(pt_ref, qbd_ref, bias_ref, kn_ref, vn_ref, *refs, heads, head_dim, scale, group):
    k_refs, v_refs = refs[:group], refs[group:2 * group]
    o_ref, c_ref, acc_ref = refs[2 * group:]
    p = pl.program_id(1)
    rows, page = qbd_ref.shape[1], kn_ref.shape[1]
    qbd = qbd_ref[0]
    bias = bias_ref[...]
    upper = _strict_upper(page)

    @pl.when(p == 0)
    def _():
        t_of_row = lax.broadcasted_iota(jnp.int32, (rows, page), 0) // heads
        mask = lax.broadcasted_iota(jnp.int32, (rows, page), 1) < t_of_row
        a, c = _sb_tile(_dot_nt(qbd, kn_ref[0]), scale, bias, upper, jnp.zeros((rows, 1), F32), mask)
        c_ref[...] = c
        acc_ref[...] = _dot(a.astype(BF16), vn_ref[0])

    gs = range(group)
    s = [_dot_nt(qbd, _page_matrix(k_refs[g], heads)) for g in gs]
    lg = [_sb_logits(s[g], scale, bias, None) for g in gs]
    after = [_sb_after(lg[g][1], upper) for g in gs]
    c = c_ref[...]
    acc = acc_ref[...]
    for g in gs:
        a, c = _sb_weights(lg[g][0], lg[g][1], after[g], c, None)
        acc = acc + _dot(a.astype(BF16), _page_matrix(v_refs[g], heads))
    c_ref[...] = c
    acc_ref[...] = acc

    @pl.when(p == pl.num_programs(1) - 1)
    def _():
        width = heads * head_dim
        lane_head = lax.broadcasted_iota(jnp.int32, (heads, width), 1) // head_dim
        own = lane_head == lax.broadcasted_iota(jnp.int32, (heads, width), 0)
        acc = acc_ref[...].reshape(rows // heads, heads, width)
        o_ref[0] = jnp.sum(jnp.where(own[None], acc, 0.0), axis=1)


def _sb_sample(q, k_new, v_new, cache_k, cache_v, page_table, sb_bias, heads, head_dim, group):
    db, t, width = q.shape
    n_pages = page_table.shape[1]
    page = cache_k.shape[1] // heads
    assert n_pages % group == 0
    rows = t * heads
    lane_head = jnp.arange(width) // head_dim
    own = (lane_head[None, :] == jnp.arange(heads)[:, None])
    qbd = jnp.where(own[None, None], q[:, :, None, :], jnp.zeros((), q.dtype)).reshape(db, rows, width)
    bias_rows = jnp.tile(sb_bias.astype(F32), t).reshape(rows, 1)
    pad = lambda a: jnp.pad(a, ((0, 0), (0, page - t), (0, 0)))
    new_spec = pl.BlockSpec((1, page, width), lambda b, p, pt: (b, 0, 0))
    page_specs = [pl.BlockSpec((1, page * heads, head_dim),
                               lambda b, p, pt, g=g: (pt[b, n_pages - 1 - (p * group + g)], 0, 0))
                  for g in range(group)]
    return pl.pallas_call(
        functools.partial(_sb_sample_kernel, heads=heads, head_dim=head_dim, scale=head_dim ** -0.5, group=group),
        out_shape=jax.ShapeDtypeStruct((db, t, width), F32),
        grid_spec=pltpu.PrefetchScalarGridSpec(
            num_scalar_prefetch=1,
            grid=(db, n_pages // group),
            in_specs=[pl.BlockSpec((1, rows, width), lambda b, p, pt: (b, 0, 0)),
                      pl.BlockSpec((rows, 1), lambda b, p, pt: (0, 0)),
                      new_spec, new_spec] + page_specs + page_specs,
            out_specs=pl.BlockSpec((1, t, width), lambda b, p, pt: (b, 0, 0)),
            scratch_shapes=[pltpu.VMEM((rows, 1), F32), pltpu.VMEM((rows, width), F32)],
        ),
        compiler_params=_params("parallel", "arbitrary"),
        name="sb_sample",
    )(page_table, qbd, bias_rows, pad(k_new), pad(v_new), *([cache_k] * group), *([cache_v] * group))


def _pool_groups(ext, tok, band_fn, inv_cnt_fn, wp_ref, scale_ref, o_ref):
    group = wp_ref.shape[1]
    for gi, w in enumerate(POOL_WINDOWS):
        cols = slice(gi * group, (gi + 1) * group)
        band = band_fn(w)
        hi, mid, lo = _split3(ext[:, cols])
        wsum = _dot(band, hi) + _dot(band, mid) + _dot(band, lo)
        d = wsum * inv_cnt_fn(w) - tok[:, cols]
        o_ref[:, cols] = (_dot(d.astype(BF16), wp_ref[gi]) * scale_ref[:, cols]).astype(o_ref.dtype)


def _pool_prompt_kernel(u_ref, prev_ref, wp_ref, scale_ref, o_ref, *, tm, halo):
    i = pl.program_id(0)
    tok = u_ref[...]
    prev = jnp.where(i > 0, prev_ref[...], 0.0)
    ext = jnp.concatenate([prev, tok], axis=0)
    r = lax.broadcasted_iota(jnp.int32, (tm, tm + halo), 0)
    cc = lax.broadcasted_iota(jnp.int32, (tm, tm + halo), 1) - halo
    pos = i * tm + lax.broadcasted_iota(jnp.int32, (tm, 1), 0)

    band_fn = lambda w: ((cc <= r) & (cc > r - w)).astype(BF16)
    inv_cnt_fn = lambda w: 1.0 / jnp.minimum(pos + 1, w).astype(F32)
    _pool_groups(ext, tok, band_fn, inv_cnt_fn, wp_ref, scale_ref, o_ref)


def _pool_prompt(u, w_pool_bf16, pool_scale, tm):
    s, width = u.shape
    halo = 128
    assert halo > POOL_STATE and tm % halo == 0
    return pl.pallas_call(
        functools.partial(_pool_prompt_kernel, tm=tm, halo=halo),
        out_shape=jax.ShapeDtypeStruct((s, width), BF16),
        grid=(s // tm,),
        in_specs=[pl.BlockSpec((tm, width), lambda i: (i, 0)),
                  pl.BlockSpec((halo, width), lambda i: (jnp.maximum(i * (tm // halo) - 1, 0), 0)),
                  pl.BlockSpec(w_pool_bf16.shape, lambda i: (0, 0, 0)),
                  pl.BlockSpec((1, width), lambda i: (0, 0))],
        out_specs=pl.BlockSpec((tm, width), lambda i: (i, 0)),
        compiler_params=_params("parallel"),
        name="pool_prompt",
    )(u, u, w_pool_bf16, pool_scale)


def _pool_sample_kernel(buf_ref, wp_ref, scale_ref, o_ref, *, t_new, buf_len, last_pos):
    ext = buf_ref[...]
    n_out = o_ref.shape[0]
    r = lax.broadcasted_iota(jnp.int32, (n_out, ext.shape[0]), 0)
    cc = lax.broadcasted_iota(jnp.int32, (n_out, ext.shape[0]), 1)
    own_row = (r // t_new) * buf_len + (buf_len - t_new) + r % t_new
    pick = (cc == own_row).astype(BF16)
    tok = sum(_dot(pick, part) for part in _split3(ext))
    pos = last_pos - (t_new - 1) + lax.broadcasted_iota(jnp.int32, (n_out, 1), 0) % t_new

    band_fn = lambda w: ((cc <= own_row) & (cc > own_row - w)).astype(BF16)
    inv_cnt_fn = lambda w: 1.0 / jnp.minimum(pos + 1, w).astype(F32)
    _pool_groups(ext, tok, band_fn, inv_cnt_fn, wp_ref, scale_ref, o_ref)


def _pool_sample(buf, w_pool_bf16, pool_scale, t_new, last_pos):
    db, buf_len, width = buf.shape
    lead = max(POOL_WINDOWS) - (buf_len - t_new) - 1
    assert lead >= 0
    while (db * (buf_len + lead)) % 128:
        lead += 1
    padded = jnp.pad(buf, ((0, 0), (lead, 0), (0, 0)))
    full = lambda shape: pl.BlockSpec(shape, lambda i: (0,) * len(shape))
    return pl.pallas_call(
        functools.partial(_pool_sample_kernel, t_new=t_new, buf_len=buf_len + lead, last_pos=last_pos),
        out_shape=jax.ShapeDtypeStruct((db * t_new, width), BF16),
        grid=(1,),
        in_specs=[full((db * (buf_len + lead), width)), full(w_pool_bf16.shape), full((1, width))],
        out_specs=full((db * t_new, width)),
        compiler_params=_params("arbitrary"),
        name="pool_sample",
    )(padded.reshape(db * (buf_len + lead), width), w_pool_bf16, pool_scale)


def _outproj_kernel(x_ref, att_ref, pool_ref, wa_ref, wp_ref, o_ref):
    o_ref[...] = x_ref[...] + _dot(att_ref[...], wa_ref[...]) + _dot(pool_ref[...], wp_ref[...])


def _outproj(x, att, pool, w_o_bf16, tm):
    n, d = x.shape
    half = att.shape[1]
    return pl.pallas_call(
        _outproj_kernel,
        out_shape=jax.ShapeDtypeStruct((n, d), F32),
        grid=(n // tm,),
        in_specs=[pl.BlockSpec((tm, d), lambda i: (i, 0)),
                  pl.BlockSpec((tm, half), lambda i: (i, 0)),
                  pl.BlockSpec((tm, half), lambda i: (i, 0)),
                  pl.BlockSpec((half, d), lambda i: (0, 0)),
                  pl.BlockSpec((half, d), lambda i: (1, 0))],
        out_specs=pl.BlockSpec((tm, d), lambda i: (i, 0)),
        compiler_params=_params("parallel"),
        name="outproj",
    )(x, att, pool, w_o_bf16, w_o_bf16)


def _rank_lt(x, k):
    nrows = x.shape[0]
    ridx = lax.broadcasted_iota(jnp.int32, x.shape, 0)
    rank = jnp.zeros(x.shape, jnp.int32)
    for e in range(nrows):
        row = x[e:e + 1, :]
        ahead = (row > x) | ((row == x) & (e < ridx))
        rank = rank + ahead.astype(jnp.int32)
    return rank < k


def _router_kernel(t_ref, g_ref, wr_ref, rb_ref, hn_ref, eid_ref, rank_ref, w_ref, cnt_ref):
    @pl.when(pl.program_id(0) == 0)
    def _():
        cnt_ref[...] = jnp.zeros(cnt_ref.shape, F32)

    hn = _rms(t_ref[...], g_ref[...])
    hn_ref[...] = hn
    h_hi, h_mid, _ = _split3(hn)
    w_hi, w_mid, _ = _split3(wr_ref[...])
    logits = _dot_nt(w_hi, h_hi) + _dot_nt(w_hi, h_mid) + _dot_nt(w_mid, h_hi)
    s = jax.nn.sigmoid(logits)
    sb = s + rb_ref[...]
    n_exp, tm = sb.shape
    per_group = n_exp // N_EXPERT_GROUPS
    g3 = sb.reshape(N_EXPERT_GROUPS, per_group, tm)
    m1 = jnp.max(g3, axis=1, keepdims=True)
    is_top = g3 == m1
    n_top = jnp.sum(is_top.astype(jnp.int32), axis=1, keepdims=True)
    m2 = jnp.where(n_top > 1, m1, jnp.max(jnp.where(is_top, -jnp.inf, g3), axis=1, keepdims=True))
    gscore = (m1 + m2).reshape(N_EXPERT_GROUPS, tm)
    gsel = _rank_lt(gscore, TOPK_GROUPS)
    emask = jnp.broadcast_to(gsel[:, None, :], g3.shape).reshape(n_exp, tm)
    sel = _rank_lt(jnp.where(emask, sb, -jnp.inf), TOP_K) & emask
    wsel = jnp.where(sel, s, 0.0)
    comb = wsel / jnp.sum(wsel, axis=0, keepdims=True) * ROUTED_SCALE

    self = jnp.where(sel, 1.0, 0.0)
    selb = self.astype(BF16)
    lower = (lax.broadcasted_iota(jnp.int32, (n_exp, n_exp), 0)
             > lax.broadcasted_iota(jnp.int32, (n_exp, n_exp), 1)).astype(BF16)
    earlier = (lax.broadcasted_iota(jnp.int32, (tm, tm), 0)
               < lax.broadcasted_iota(jnp.int32, (tm, tm), 1)).astype(BF16)
    slot = _dot(lower, selb)
    rank = _dot(selb, earlier) + cnt_ref[:, 0:1]
    eidx = lax.broadcasted_iota(jnp.int32, (n_exp, tm), 0).astype(F32)
    pick = lambda m, v: jnp.sum(jnp.where(m, v, 0.0), axis=0, keepdims=True)
    eids, ranks, ws = [], [], []
    for k in range(TOP_K):
        m = sel & (slot == float(k))
        eids.append(pick(m, eidx))
        ranks.append(pick(m, rank))
        ws.append(pick(m, comb))
    eid_ref[...] = jnp.concatenate(eids, axis=0).astype(jnp.int32)
    rank_ref[...] = jnp.concatenate(ranks, axis=0).astype(jnp.int32)
    w_ref[...] = jnp.concatenate(ws, axis=0)
    cnt_ref[...] += jnp.sum(self, axis=1, keepdims=True)


def _router(t, g, w_router_t, router_bias, tm):
    n, d = t.shape
    n_exp = w_router_t.shape[0]
    slots = lambda dt: jax.ShapeDtypeStruct((TOP_K, n), dt)
    slot_spec = pl.BlockSpec((TOP_K, tm), lambda i: (0, i))
    return pl.pallas_call(
        _router_kernel,
        out_shape=(jax.ShapeDtypeStruct((n, d), F32), slots(jnp.int32), slots(jnp.int32), slots(F32),
                   jax.ShapeDtypeStruct((n_exp, 128), F32)),
        grid=(n // tm,),
        in_specs=[pl.BlockSpec((tm, d), lambda i: (i, 0)),
                  pl.BlockSpec((1, d), lambda i: (0, 0)),
                  pl.BlockSpec((n_exp, d), lambda i: (0, 0)),
                  pl.BlockSpec((n_exp, 1), lambda i: (0, 0))],
        out_specs=(pl.BlockSpec((tm, d), lambda i: (i, 0)), slot_spec, slot_spec, slot_spec,
                   pl.BlockSpec((n_exp, 128), lambda i: (0, 0))),
        compiler_params=_params("arbitrary"),
        name="router",
    )(t, g, w_router_t, router_bias)


def _row_copies(n_rows, make_copy, wait):
    def body(n, carry):
        for k in range(TOP_K):
            cp = make_copy(n, k)
            cp.wait() if wait else cp.start()
        return carry
    lax.fori_loop(0, n_rows, body, 0)


def _dispatch_kernel(pos_ref, hn_ref, xs_ref, sem):
    make_copy = lambda n, k: pltpu.make_async_copy(
        hn_ref.at[pl.ds(n, 1)], xs_ref.at[pl.ds(pos_ref[k, n], 1)], sem)
    _row_copies(hn_ref.shape[0], make_copy, wait=False)
    _row_copies(hn_ref.shape[0], make_copy, wait=True)


def _dispatch(hn, pos, tm):
    n, d = hn.shape
    return pl.pallas_call(
        _dispatch_kernel,
        out_shape=jax.ShapeDtypeStruct((n * TOP_K, d), F32),
        grid=(n // tm,),
        in_specs=[pl.BlockSpec((TOP_K, tm), lambda i: (0, i), memory_space=pltpu.SMEM),
                  pl.BlockSpec((tm, d), lambda i: (i, 0))],
        out_specs=pl.BlockSpec(memory_space=pl.ANY),
        scratch_shapes=[pltpu.SemaphoreType.DMA(())],
        compiler_params=_params("arbitrary"),
        name="moe_dispatch",
    )(pos, hn)


def _swiglu(h, wg, wu):
    gate = _dot(h, wg)
    return gate * jax.nn.sigmoid(gate) * _dot(h, wu)


def _experts_kernel(tile_ref, exp_ref, lo_ref, hi_ref, xs_ref, wg_ref, wu_ref, wd_ref, y_ref, wgb, wub, wdb):
    v = pl.program_id(0)
    lo, hi = lo_ref[v], hi_ref[v]
    new_expert = jnp.logical_or(v == 0, exp_ref[v] != exp_ref[jnp.maximum(v - 1, 0)])

    @pl.when(new_expert)
    def _():
        wgb[...] = wg_ref[0].astype(BF16)
        wub[...] = wu_ref[0].astype(BF16)
        wdb[...] = wd_ref[0].astype(BF16)

    @pl.when(hi > lo)
    def _():
        y = _dot(_swiglu(xs_ref[...].astype(BF16), wgb[...], wub[...]).astype(BF16), wdb[...])

        @pl.when(lo == 0)
        def _():
            y_ref[...] = y

        @pl.when(lo > 0)
        def _():
            own = lax.broadcasted_iota(jnp.int32, (y.shape[0], 1), 0) >= lo
            y_ref[...] = jnp.where(own, y, y_ref[...])


def _experts(xs, visits, wg, wu, wd, tm):
    r, d = xs.shape
    n_exp, _, de = wg.shape
    n_visits = visits[0].shape[0]
    return pl.pallas_call(
        _experts_kernel,
        out_shape=jax.ShapeDtypeStruct((r, d), F32),
        grid_spec=pltpu.PrefetchScalarGridSpec(
            num_scalar_prefetch=4,
            grid=(n_visits,),
            in_specs=[pl.BlockSpec((tm, d), lambda v, t, e, lo, hi: (t[v], 0)),
                      pl.BlockSpec((1, d, de), lambda v, t, e, lo, hi: (e[v], 0, 0)),
                      pl.BlockSpec((1, d, de), lambda v, t, e, lo, hi: (e[v], 0, 0)),
                      pl.BlockSpec((1, de, d), lambda v, t, e, lo, hi: (e[v], 0, 0))],
            out_specs=pl.BlockSpec((tm, d), lambda v, t, e, lo, hi: (t[v], 0)),
            scratch_shapes=[pltpu.VMEM((d, de), BF16), pltpu.VMEM((d, de), BF16), pltpu.VMEM((de, d), BF16)],
        ),
        compiler_params=_params("arbitrary"),
        name="moe_experts",
    )(*visits, xs, wg, wu, wd)


def _visit_plan(counts, n_rows, tm):
    n_exp = counts.shape[0]
    ends = jnp.cumsum(counts)
    starts = ends - counts
    first_tile = starts // tm
    n_vis = jnp.where(counts > 0, (ends - 1) // tm - first_tile + 1, 0)
    vis_end = jnp.cumsum(n_vis)
    total = vis_end[-1]
    n_visits = n_rows // tm + n_exp - 1
    v = jnp.minimum(jnp.arange(n_visits, dtype=jnp.int32), total - 1)
    e = jnp.sum(v[:, None] >= vis_end[None, :], axis=1).astype(jnp.int32)
    tile = (first_tile[e] + v - (vis_end[e] - n_vis[e])).astype(jnp.int32)
    lo = jnp.maximum(starts[e], tile * tm) - tile * tm
    hi = jnp.minimum(ends[e], (tile + 1) * tm) - tile * tm
    hi = jnp.where(jnp.arange(n_visits) < total, hi, lo)
    return (tile, e, lo.astype(jnp.int32), hi.astype(jnp.int32)), starts


def _combine_kernel(pos_ref, nxt_ref, w_ref, hn_ref, sg_ref, su_ref, sd_ref, y_ref, o_ref, buf, sem):
    i = pl.program_id(0)
    n_steps = pl.num_programs(0)
    tm = hn_ref.shape[0]

    def gather(p_ref, slot, wait):
        make_copy = lambda n, k: pltpu.make_async_copy(
            y_ref.at[pl.ds(p_ref[k, n], 1)], buf.at[slot, k, pl.ds(n, 1)], sem.at[slot])
        _row_copies(tm, make_copy, wait)

    @pl.when(i == 0)
    def _():
        gather(pos_ref, 0, wait=False)

    @pl.when(i + 1 < n_steps)
    def _():
        gather(nxt_ref, (i + 1) % 2, wait=False)

    acc = _dot(_swiglu(hn_ref[...].astype(BF16), sg_ref[...], su_ref[...]).astype(BF16), sd_ref[...])
    gather(pos_ref, i % 2, wait=True)
    w = w_ref[...]
    for k in range(TOP_K):
        acc = acc + w[:, k:k + 1] * buf[i % 2, k]
    o_ref[...] = acc


def _combine(y, pos, w_tok, hn, sg, su, sd, tm):
    n, d = hn.shape
    ds = sg.shape[1]
    n_steps = n // tm
    full = lambda a, b: pl.BlockSpec((a, b), lambda i: (0, 0))
    return pl.pallas_call(
        _combine_kernel,
        out_shape=jax.ShapeDtypeStruct((n, d), F32),
        grid=(n_steps,),
        in_specs=[pl.BlockSpec((TOP_K, tm), lambda i: (0, i), memory_space=pltpu.SMEM),
                  pl.BlockSpec((TOP_K, tm), lambda i: (0, jnp.minimum(i + 1, n_steps - 1)), memory_space=pltpu.SMEM),
                  pl.BlockSpec((tm, TOP_K), lambda i: (i, 0)),
                  pl.BlockSpec((tm, d), lambda i: (i, 0)),
                  full(d, ds), full(d, ds), full(ds, d),
                  pl.BlockSpec(memory_space=pl.ANY)],
        out_specs=pl.BlockSpec((tm, d), lambda i: (i, 0)),
        scratch_shapes=[pltpu.VMEM((2, TOP_K, tm, d), F32), pltpu.SemaphoreType.DMA((2,))],
        compiler_params=_params("arbitrary"),
        name="moe_combine",
    )(pos, pos, w_tok, hn, sg, su, sd, y)


def _ple_kernel(t_ref, m_ref, p_ref, gp_ref, wg_ref, wp_ref, gf_ref, o_ref):
    t = t_ref[...] + m_ref[...]
    gate = jax.nn.sigmoid(_dot(_rms(t, gp_ref[...]).astype(BF16), wg_ref[...]))
    pe = _dot(p_ref[...].astype(BF16), wp_ref[...])
    o_ref[...] = _rms(t + pe * gate, gf_ref[...])


def _ple(t, m, p, g_ple, w_gate_bf16, w_proj_bf16, g_final, tm):
    n, d = t.shape
    dp = p.shape[1]
    tile = lambda w: pl.BlockSpec((tm, w), lambda i: (i, 0))
    full = lambda a, b: pl.BlockSpec((a, b), lambda i: (0, 0))
    return pl.pallas_call(
        _ple_kernel,
        out_shape=jax.ShapeDtypeStruct((n, d), F32),
        grid=(n // tm,),
        in_specs=[tile(d), tile(d), tile(dp), full(1, d), full(d, d), full(dp, d), full(1, d)],
        out_specs=tile(d),
        compiler_params=_params("parallel"),
        name="ple_final",
    )(t, m, p, g_ple, w_gate_bf16, w_proj_bf16, g_final)


def _tile(n, candidates):
    for c in candidates:
        if n % c == 0:
            return c
    raise ValueError(f"no tile for {n}")


def kernel(x_prompt, x_sample, cache_k, cache_v, state_pool, page_table, p_prompt, p_sample, g_mix, w_in, sb_bias, w_pool, pool_scale, w_o, g_ffn, w_router, router_bias, w_exp_gate, w_exp_up, w_exp_down, w_sh_gate, w_sh_up, w_sh_down, g_ple, w_ple_gate, w_ple_proj, g_final):
    depth = w_in.shape[0]
    assert depth == 1 and x_prompt.shape[0] == 1
    _, seq, d = x_prompt.shape
    db, t_new, _ = x_sample.shape
    _, n_phys, page, heads, head_dim = cache_k.shape
    sbw = heads * head_dim
    past_len = page_table.shape[1] * page
    row = lambda a: a.reshape(1, -1).astype(F32)

    xp = x_prompt.reshape(seq, d)
    xs = x_sample.reshape(db * t_new, d)
    w_in_b = w_in[0].astype(BF16)

    qp, kp, vp, up, kpb, vpb = _inproj(xp, row(g_mix[0]), w_in_b, _tile(seq, (512, 256, 128)))
    qs, ks, vs, us, ksb, vsb = _inproj(xs, row(g_mix[0]), w_in_b, db * t_new)

    att_p = _sb_prompt(qp, kpb, vpb, sb_bias[0].astype(F32), head_dim, 256, 256, 4)
    att_s = _sb_sample(qs.reshape(db, t_new, sbw), ksb.reshape(db, t_new, sbw), vsb.reshape(db, t_new, sbw),
                       cache_k.reshape(n_phys, page * heads, head_dim), cache_v.reshape(n_phys, page * heads, head_dim),
                       page_table, sb_bias[0], heads, head_dim, _tile(page_table.shape[1], (8, 4, 2, 1)))

    w_pool_b = w_pool[0].astype(BF16)
    pool_p = _pool_prompt(up, w_pool_b, row(pool_scale[0]), 256)
    buf = jnp.concatenate([state_pool[0], us.reshape(db, t_new, -1)], axis=1)
    pool_s = _pool_sample(buf, w_pool_b, row(pool_scale[0]), t_new, past_len + t_new - 1)

    x_all = jnp.concatenate([xp, xs], axis=0)
    att_all = jnp.concatenate([att_p, att_s.reshape(db * t_new, sbw).astype(BF16)], axis=0)
    pool_all = jnp.concatenate([pool_p, pool_s], axis=0)
    n = x_all.shape[0]
    tm = _tile(n, (640, 512, 256, 128))
    t1 = _outproj(x_all, att_all, pool_all, w_o[0].astype(BF16), tm)

    hn, eid, rank, w_slot, counts = _router(t1, row(g_ffn[0]), w_router[0].T,
                                            router_bias[0].reshape(-1, 1).astype(F32), tm)
    row_tile = _tile(n * TOP_K, (256, 128))
    visits, starts = _visit_plan(counts[:, 0].astype(jnp.int32), n * TOP_K, row_tile)
    pos = starts[eid] + rank
    xs_sorted = _dispatch(hn, pos, _tile(n, (128,)))
    y_sorted = _experts(xs_sorted, visits, w_exp_gate[0], w_exp_up[0], w_exp_down[0], row_tile)
    routed = _combine(y_sorted, pos, w_slot.T, hn, w_sh_gate[0].astype(BF16), w_sh_up[0].astype(BF16),
                      w_sh_down[0].astype(BF16), _tile(n, (128,)))

    p_all = jnp.concatenate([p_prompt[0].reshape(seq, -1), p_sample[0].reshape(db * t_new, -1)], axis=0)
    y = _ple(t1, routed, p_all, row(g_ple[0]), w_ple_gate[0].astype(BF16), w_ple_proj[0].astype(BF16),
             row(g_final), _tile(n, (320, 256, 128)))

    heads5 = lambda a, b, l: a.reshape(1, b, l, heads, head_dim)
    return (y[:seq].reshape(x_prompt.shape), y[seq:].reshape(x_sample.shape),
            heads5(kp, 1, seq), heads5(vp, 1, seq), up[None, None, seq - POOL_STATE:, :],
            heads5(ks, db, t_new), heads5(vs, db, t_new), buf[None, :, -POOL_STATE:, :])
```

```python
import functools

import jax
import jax.numpy as jnp
from jax import lax
from jax.experimental import pallas as pl
from jax.experimental.pallas import tpu as pltpu

F32 = jnp.float32
BF16 = jnp.bfloat16

RMS_EPS = 1e-6
POOL_WINDOWS = (2, 4, 8, 16)
POOL_STATE = max(POOL_WINDOWS) - 1
TOP_K = 8
N_EXPERT_GROUPS = 8
TOPK_GROUPS = 4
ROUTED_SCALE = 2.5

V7X_VMEM_LIMIT_BYTES = 56 * 1024 * 1024


def _params(*sem):
    return pltpu.CompilerParams(dimension_semantics=sem, vmem_limit_bytes=V7X_VMEM_LIMIT_BYTES)


def _rms(x, g):
    return x * lax.rsqrt(jnp.mean(x * x, axis=-1, keepdims=True) + RMS_EPS) * g


def _split3(x):
    hi = x.astype(BF16)
    r = x - hi.astype(F32)
    mid = r.astype(BF16)
    lo = (r - mid.astype(F32)).astype(BF16)
    return hi, mid, lo


def _dot(a, b):
    return jnp.dot(a, b, preferred_element_type=F32)


def _dot_nt(a, b):
    return lax.dot_general(a, b, (((1,), (1,)), ((), ())), preferred_element_type=F32)


def _inproj_kernel(x_ref, g_ref, w_ref, q_ref, k_ref, v_ref, u_ref, kb_ref, vb_ref, xn_ref):
    j = pl.program_id(1)

    @pl.when(j == 0)
    def _():
        xn_ref[...] = _rms(x_ref[...], g_ref[...]).astype(BF16)

    z = _dot(xn_ref[...], w_ref[...])

    @pl.when(j == 0)
    def _():
        q_ref[...] = z.astype(BF16)

    @pl.when(j == 1)
    def _():
        k_ref[...] = z
        kb_ref[...] = z.astype(BF16)

    @pl.when(j == 2)
    def _():
        v_ref[...] = z
        vb_ref[...] = z.astype(BF16)

    @pl.when(j == 3)
    def _():
        u_ref[...] = z


def _inproj(x, g, w_bf16, tm):
    n, d = x.shape
    wn = w_bf16.shape[1] // 4
    out = lambda dt: jax.ShapeDtypeStruct((n, wn), dt)
    ospec = pl.BlockSpec((tm, wn), lambda i, j: (i, 0))
    return pl.pallas_call(
        _inproj_kernel,
        out_shape=(out(BF16), out(F32), out(F32), out(F32), out(BF16), out(BF16)),
        grid=(n // tm, 4),
        in_specs=[pl.BlockSpec((tm, d), lambda i, j: (i, 0)),
                  pl.BlockSpec((1, d), lambda i, j: (0, 0)),
                  pl.BlockSpec((d, wn), lambda i, j: (0, j))],
        out_specs=(ospec,) * 6,
        scratch_shapes=[pltpu.VMEM((tm, d), BF16)],
        compiler_params=_params("parallel", "arbitrary"),
        name="inproj",
    )(x, g, w_bf16)


def _sb_logits(s, scale, bias, mask):
    z = s * scale + bias
    sp = jnp.maximum(z, 0.0) + jnp.log(1.0 + jnp.exp(-jnp.abs(z)))
    return z - sp, sp if mask is None else jnp.where(mask, sp, 0.0)


def _sb_after(sp, neg_upper):
    hi = sp.astype(BF16)
    lo = (sp - hi.astype(F32)).astype(BF16)
    return _dot(hi, neg_upper) + _dot(lo, neg_upper)


def _sb_weights(ls, sp, after, c, mask):
    a = jnp.exp(ls + after + c)
    if mask is not None:
        a = jnp.where(mask, a, 0.0)
    return a, c - jnp.sum(sp, axis=-1, keepdims=True)


def _sb_tile(s, scale, bias, neg_upper, c, mask):
    ls, sp = _sb_logits(s, scale, bias, mask)
    return _sb_weights(ls, sp, _sb_after(sp, neg_upper), c, mask)


def _strict_upper(t):
    later = lax.broadcasted_iota(jnp.int32, (t, t), 0) > lax.broadcasted_iota(jnp.int32, (t, t), 1)
    return jnp.where(later, -1.0, 0.0).astype(BF16)


def _sb_prompt_kernel(bias_ref, q_ref, k_ref, v_ref, o_ref, *, tq, tk, head_dim, scale):
    hg = pl.program_id(0)
    i = pl.program_id(1)
    n_heads = q_ref.shape[1] // head_dim
    lanes = [slice(h * head_dim, (h + 1) * head_dim) for h in range(n_heads)]
    bias = [bias_ref[hg * n_heads + h] for h in range(n_heads)]
    q = [q_ref[:, l] for l in lanes]
    upper = _strict_upper(tk)
    last = (i * tq) // tk
    causal = (lax.broadcasted_iota(jnp.int32, (tq, tk), 1)
              < lax.broadcasted_iota(jnp.int32, (tq, tk), 0) + (i * tq - last * tk))

    def block(j, carry, mask):
        rows = pl.ds(pl.multiple_of(j * tk, tk), tk)
        hs = range(n_heads)
        s = [_dot_nt(q[h], k_ref[rows, lanes[h]]) for h in hs]
        lg = [_sb_logits(s[h], scale, bias[h], mask) for h in hs]
        after = [_sb_after(lg[h][1], upper) for h in hs]
        w = [_sb_weights(lg[h][0], lg[h][1], after[h], carry[h][0], mask) for h in hs]
        return tuple((w[h][1], carry[h][1] + _dot(w[h][0].astype(BF16), v_ref[rows, lanes[h]])) for h in hs)

    zero = (jnp.zeros((tq, 1), F32), jnp.zeros((tq, head_dim), F32))
    carry = block(last, (zero,) * n_heads, causal)
    carry = lax.fori_loop(0, last, lambda jj, carry: block(last - 1 - jj, carry, None), carry)
    for h in range(n_heads):
        o_ref[:, lanes[h]] = carry[h][1].astype(o_ref.dtype)


def _sb_prompt(q, kb, vb, sb_bias, head_dim, tq, tk, heads_per_step):
    s, width = q.shape
    assert tk % tq == 0 and s % tk == 0
    gw = heads_per_step * head_dim
    kv_spec = pl.BlockSpec((s, gw), lambda h, i, b: (0, h))
    return pl.pallas_call(
        functools.partial(_sb_prompt_kernel, tq=tq, tk=tk, head_dim=head_dim, scale=head_dim ** -0.5),
        out_shape=jax.ShapeDtypeStruct((s, width), BF16),
        grid_spec=pltpu.PrefetchScalarGridSpec(
            num_scalar_prefetch=1,
            grid=(width // gw, s // tq),
            in_specs=[pl.BlockSpec((tq, gw), lambda h, i, b: (i, h)), kv_spec, kv_spec],
            out_specs=pl.BlockSpec((tq, gw), lambda h, i, b: (i, h)),
        ),
        compiler_params=_params("parallel", "arbitrary"),
        name="sb_prompt",
    )(sb_bias, q, kb, vb)


def _page_matrix(page_ref, heads):
    keys = page_ref.shape[1] // heads
    cols = [page_ref[0, pl.ds(h, keys, stride=heads), :] for h in range(heads)]
    return jnp.concatenate(cols, axis=1).astype(BF16)


def _sb_sample_kernel(pt_ref, qbd_ref, bias_ref, kn_ref, vn_ref, *refs, heads, head_dim, scale, group):
    k_refs, v_refs = refs[:group], refs[group:2 * group]
    o_ref, c_ref, acc_ref = refs[2 * group:]
    p = pl.program_id(1)
    rows, page = qbd_ref.shape[1], kn_ref.shape[1]
    qbd = qbd_ref[0]
    bias = bias_ref[...]
    upper = _strict_upper(page)

    @pl.when(p == 0)
    def _():
        t_of_row = lax.broadcasted_iota(jnp.int32, (rows, page), 0) // heads
        mask = lax.broadcasted_iota(jnp.int32, (rows, page), 1) < t_of_row
        a, c = _sb_tile(_dot_nt(qbd, kn_ref[0]), scale, bias, upper, jnp.zeros((rows, 1), F32), mask)
        c_ref[...] = c
        acc_ref[...] = _dot(a.astype(BF16), vn_ref[0])

    gs = range(group)
    s = [_dot_nt(qbd, _page_matrix(k_refs[g], heads)) for g in gs]
    lg = [_sb_logits(s[g], scale, bias, None) for g in gs]
    after = [_sb_after(lg[g][1], upper) for g in gs]
    c = c_ref[...]
    acc = acc_ref[...]
    for g in gs:
        a, c = _sb_weights(lg[g][0], lg[g][1], after[g], c, None)
        acc = acc + _dot(a.astype(BF16), _page_matrix(v_refs[g], heads))
    c_ref[...] = c
    acc_ref[...] = acc

    @pl.when(p == pl.num_programs(1) - 1)
    def _():
        width = heads * head_dim
        lane_head = lax.broadcasted_iota(jnp.int32, (heads, width), 1) // head_dim
        own = lane_head == lax.broadcasted_iota(jnp.int32, (heads, width), 0)
        acc = acc_ref[...].reshape(rows // heads, heads, width)
        o_ref[0] = jnp.sum(jnp.where(own[None], acc, 0.0), axis=1)


def _sb_sample(q, k_new, v_new, cache_k, cache_v, page_table, sb_bias, heads, head_dim, group):
    db, t, width = q.shape
    n_pages = page_table.shape[1]
    page = cache_k.shape[1] // heads
    assert n_pages % group == 0
    rows = t * heads
    lane_head = jnp.arange(width) // head_dim
    own = (lane_head[None, :] == jnp.arange(heads)[:, None])
    qbd = jnp.where(own[None, None], q[:, :, None, :], jnp.zeros((), q.dtype)).reshape(db, rows, width)
    bias_rows = jnp.tile(sb_bias.astype(F32), t).reshape(rows, 1)
    pad = lambda a: jnp.pad(a, ((0, 0), (0, page - t), (0, 0)))
    new_spec = pl.BlockSpec((1, page, width), lambda b, p, pt: (b, 0, 0))
    page_specs = [pl.BlockSpec((1, page * heads, head_dim),
                               lambda b, p, pt, g=g: (pt[b, n_pages - 1 - (p * group + g)], 0, 0))
                  for g in range(group)]
    return pl.pallas_call(
        functools.partial(_sb_sample_kernel, heads=heads, head_dim=head_dim, scale=head_dim ** -0.5, group=group),
        out_shape=jax.ShapeDtypeStruct((db, t, width), F32),
        grid_spec=pltpu.PrefetchScalarGridSpec(
            num_scalar_prefetch=1,
            grid=(db, n_pages // group),
            in_specs=[pl.BlockSpec((1, rows, width), lambda b, p, pt: (b, 0, 0)),
                      pl.BlockSpec((rows, 1), lambda b, p, pt: (0, 0)),
                      new_spec, new_spec] + page_specs + page_specs,
            out_specs=pl.BlockSpec((1, t, width), lambda b, p, pt: (b, 0, 0)),
            scratch_shapes=[pltpu.VMEM((rows, 1), F32), pltpu.VMEM((rows, width), F32)],
        ),
        compiler_params=_params("parallel", "arbitrary"),
        name="sb_sample",
    )(page_table, qbd, bias_rows, pad(k_new), pad(v_new), *([cache_k] * group), *([cache_v] * group))


def _pool_groups(ext, tok, band_fn, inv_cnt_fn, wp_ref, scale_ref, o_ref):
    group = wp_ref.shape[1]
    for gi, w in enumerate(POOL_WINDOWS):
        cols = slice(gi * group, (gi + 1) * group)
        band = band_fn(w)
        hi, mid, lo = _split3(ext[:, cols])
        wsum = _dot(band, hi) + _dot(band, mid) + _dot(band, lo)
        d = wsum * inv_cnt_fn(w) - tok[:, cols]
        o_ref[:, cols] = (_dot(d.astype(BF16), wp_ref[gi]) * scale_ref[:, cols]).astype(o_ref.dtype)


def _pool_prompt_kernel(u_ref, prev_ref, wp_ref, scale_ref, o_ref, *, tm, halo):
    i = pl.program_id(0)
    tok = u_ref[...]
    prev = jnp.where(i > 0, prev_ref[...], 0.0)
    ext = jnp.concatenate([prev, tok], axis=0)
    r = lax.broadcasted_iota(jnp.int32, (tm, tm + halo), 0)
    cc = lax.broadcasted_iota(jnp.int32, (tm, tm + halo), 1) - halo
    pos = i * tm + lax.broadcasted_iota(jnp.int32, (tm, 1), 0)

    band_fn = lambda w: ((cc <= r) & (cc > r - w)).astype(BF16)
    inv_cnt_fn = lambda w: 1.0 / jnp.minimum(pos + 1, w).astype(F32)
    _pool_groups(ext, tok, band_fn, inv_cnt_fn, wp_ref, scale_ref, o_ref)


def _pool_prompt(u, w_pool_bf16, pool_scale, tm):
    s, width = u.shape
    halo = 128
    assert halo > POOL_STATE and tm % halo == 0
    return pl.pallas_call(
        functools.partial(_pool_prompt_kernel, tm=tm, halo=halo),
        out_shape=jax.ShapeDtypeStruct((s, width), BF16),
        grid=(s // tm,),
        in_specs=[pl.BlockSpec((tm, width), lambda i: (i, 0)),
                  pl.BlockSpec((halo, width), lambda i: (jnp.maximum(i * (tm // halo) - 1, 0), 0)),
                  pl.BlockSpec(w_pool_bf16.shape, lambda i: (0, 0, 0)),
                  pl.BlockSpec((1, width), lambda i: (0, 0))],
        out_specs=pl.BlockSpec((tm, width), lambda i: (i, 0)),
        compiler_params=_params("parallel"),
        name="pool_prompt",
    )(u, u, w_pool_bf16, pool_scale)


def _pool_sample_kernel(buf_ref, wp_ref, scale_ref, o_ref, *, t_new, buf_len, last_pos):
    ext = buf_ref[...]
    n_out = o_ref.shape[0]
    r = lax.broadcasted_iota(jnp.int32, (n_out, ext.shape[0]), 0)
    cc = lax.broadcasted_iota(jnp.int32, (n_out, ext.shape[0]), 1)
    own_row = (r // t_new) * buf_len + (buf_len - t_new) + r % t_new
    pick = (cc == own_row).astype(BF16)
    tok = sum(_dot(pick, part) for part in _split3(ext))
    pos = last_pos - (t_new - 1) + lax.broadcasted_iota(jnp.int32, (n_out, 1), 0) % t_new

    band_fn = lambda w: ((cc <= own_row) & (cc > own_row - w)).astype(BF16)
    inv_cnt_fn = lambda w: 1.0 / jnp.minimum(pos + 1, w).astype(F32)
    _pool_groups(ext, tok, band_fn, inv_cnt_fn, wp_ref, scale_ref, o_ref)


def _pool_sample(buf, w_pool_bf16, pool_scale, t_new, last_pos):
    db, buf_len, width = buf.shape
    lead = max(POOL_WINDOWS) - (buf_len - t_new) - 1
    assert lead >= 0
    while (db * (buf_len + lead)) % 128:
        lead += 1
    padded = jnp.pad(buf, ((0, 0), (lead, 0), (0, 0)))
    full = lambda shape: pl.BlockSpec(shape, lambda i: (0,) * len(shape))
    return pl.pallas_call(
        functools.partial(_pool_sample_kernel, t_new=t_new, buf_len=buf_len + lead, last_pos=last_pos),
        out_shape=jax.ShapeDtypeStruct((db * t_new, width), BF16),
        grid=(1,),
        in_specs=[full((db * (buf_len + lead), width)), full(w_pool_bf16.shape), full((1, width))],
        out_specs=full((db * t_new, width)),
        compiler_params=_params("arbitrary"),
        name="pool_sample",
    )(padded.reshape(db * (buf_len + lead), width), w_pool_bf16, pool_scale)


def _outproj_kernel(x_ref, att_ref, pool_ref, wa_ref, wp_ref, o_ref):
    o_ref[...] = x_ref[...] + _dot(att_ref[...], wa_ref[...]) + _dot(pool_ref[...], wp_ref[...])


def _outproj(x, att, pool, w_o_bf16, tm):
    n, d = x.shape
    half = att.shape[1]
    return pl.pallas_call(
        _outproj_kernel,
        out_shape=jax.ShapeDtypeStruct((n, d), F32),
        grid=(n // tm,),
        in_specs=[pl.BlockSpec((tm, d), lambda i: (i, 0)),
                  pl.BlockSpec((tm, half), lambda i: (i, 0)),
                  pl.BlockSpec((tm, half), lambda i: (i, 0)),
                  pl.BlockSpec((half, d), lambda i: (0, 0)),
                  pl.BlockSpec((half, d), lambda i: (1, 0))],
        out_specs=pl.BlockSpec((tm, d), lambda i: (i, 0)),
        compiler_params=_params("parallel"),
        name="outproj",
    )(x, att, pool, w_o_bf16, w_o_bf16)


def _rank_lt(x, k):
    nrows = x.shape[0]
    ridx = lax.broadcasted_iota(jnp.int32, x.shape, 0)
    rank = jnp.zeros(x.shape, jnp.int32)
    for e in range(nrows):
        row = x[e:e + 1, :]
        ahead = (row > x) | ((row == x) & (e < ridx))
        rank = rank + ahead.astype(jnp.int32)
    return rank < k


def _router_kernel(t_ref, g_ref, wr_ref, rb_ref, hn_ref, eid_ref, rank_ref, w_ref, cnt_ref):
    @pl.when(pl.program_id(0) == 0)
    def _():
        cnt_ref[...] = jnp.zeros(cnt_ref.shape, F32)

    hn = _rms(t_ref[...], g_ref[...])
    hn_ref[...] = hn
    h_hi, h_mid, _ = _split3(hn)
    w_hi, w_mid, _ = _split3(wr_ref[...])
    logits = _dot_nt(w_hi, h_hi) + _dot_nt(w_hi, h_mid) + _dot_nt(w_mid, h_hi)
    s = jax.nn.sigmoid(logits)
    sb = s + rb_ref[...]
    n_exp, tm = sb.shape
    per_group = n_exp // N_EXPERT_GROUPS
    g3 = sb.reshape(N_EXPERT_GROUPS, per_group, tm)
    m1 = jnp.max(g3, axis=1, keepdims=True)
    is_top = g3 == m1
    n_top = jnp.sum(is_top.astype(jnp.int32), axis=1, keepdims=True)
    m2 = jnp.where(n_top > 1, m1, jnp.max(jnp.where(is_top, -jnp.inf, g3), axis=1, keepdims=True))
    gscore = (m1 + m2).reshape(N_EXPERT_GROUPS, tm)
    gsel = _rank_lt(gscore, TOPK_GROUPS)
    emask = jnp.broadcast_to(gsel[:, None, :], g3.shape).reshape(n_exp, tm)
    sel = _rank_lt(jnp.where(emask, sb, -jnp.inf), TOP_K) & emask
    wsel = jnp.where(sel, s, 0.0)
    comb = wsel / jnp.sum(wsel, axis=0, keepdims=True) * ROUTED_SCALE

    self = jnp.where(sel, 1.0, 0.0)
    selb = self.astype(BF16)
    lower = (lax.broadcasted_iota(jnp.int32, (n_exp, n_exp), 0)
             > lax.broadcasted_iota(jnp.int32, (n_exp, n_exp), 1)).astype(BF16)
    earlier = (lax.broadcasted_iota(jnp.int32, (tm, tm), 0)
               < lax.broadcasted_iota(jnp.int32, (tm, tm), 1)).astype(BF16)
    slot = _dot(lower, selb)
    rank = _dot(selb, earlier) + cnt_ref[:, 0:1]
    eidx = lax.broadcasted_iota(jnp.int32, (n_exp, tm), 0).astype(F32)
    pick = lambda m, v: jnp.sum(jnp.where(m, v, 0.0), axis=0, keepdims=True)
    eids, ranks, ws = [], [], []
    for k in range(TOP_K):
        m = sel & (slot == float(k))
        eids.append(pick(m, eidx))
        ranks.append(pick(m, rank))
        ws.append(pick(m, comb))
    eid_ref[...] = jnp.concatenate(eids, axis=0).astype(jnp.int32)
    rank_ref[...] = jnp.concatenate(ranks, axis=0).astype(jnp.int32)
    w_ref[...] = jnp.concatenate(ws, axis=0)
    cnt_ref[...] += jnp.sum(self, axis=1, keepdims=True)


def _router(t, g, w_router_t, router_bias, tm):
    n, d = t.shape
    n_exp = w_router_t.shape[0]
    slots = lambda dt: jax.ShapeDtypeStruct((TOP_K, n), dt)
    slot_spec = pl.BlockSpec((TOP_K, tm), lambda i: (0, i))
    return pl.pallas_call(
        _router_kernel,
        out_shape=(jax.ShapeDtypeStruct((n, d), F32), slots(jnp.int32), slots(jnp.int32), slots(F32),
                   jax.ShapeDtypeStruct((n_exp, 128), F32)),
        grid=(n // tm,),
        in_specs=[pl.BlockSpec((tm, d), lambda i: (i, 0)),
                  pl.BlockSpec((1, d), lambda i: (0, 0)),
                  pl.BlockSpec((n_exp, d), lambda i: (0, 0)),
                  pl.BlockSpec((n_exp, 1), lambda i: (0, 0))],
        out_specs=(pl.BlockSpec((tm, d), lambda i: (i, 0)), slot_spec, slot_spec, slot_spec,
                   pl.BlockSpec((n_exp, 128), lambda i: (0, 0))),
        compiler_params=_params("arbitrary"),
        name="router",
    )(t, g, w_router_t, router_bias)


def _row_copies(n_rows, make_copy, wait):
    def body(n, carry):
        for k in range(TOP_K):
            cp = make_copy(n, k)
            cp.wait() if wait else cp.start(priority=k % 2)
        return carry
    lax.fori_loop(0, n_rows, body, 0)


def _dispatch_kernel(pos_ref, hn_ref, xs_ref, sem):
    make_copy = lambda n, k: pltpu.make_async_copy(
        hn_ref.at[pl.ds(n, 1)], xs_ref.at[pl.ds(pos_ref[k, n], 1)], sem)
    _row_copies(hn_ref.shape[0], make_copy, wait=False)
    _row_copies(hn_ref.shape[0], make_copy, wait=True)


def _dispatch(hn, pos, tm):
    n, d = hn.shape
    return pl.pallas_call(
        _dispatch_kernel,
        out_shape=jax.ShapeDtypeStruct((n * TOP_K, d), F32),
        grid=(n // tm,),
        in_specs=[pl.BlockSpec((TOP_K, tm), lambda i: (0, i), memory_space=pltpu.SMEM),
                  pl.BlockSpec((tm, d), lambda i: (i, 0))],
        out_specs=pl.BlockSpec(memory_space=pl.ANY),
        scratch_shapes=[pltpu.SemaphoreType.DMA(())],
        compiler_params=_params("arbitrary"),
        name="moe_dispatch",
    )(pos, hn)


def _swiglu(h, wg, wu):
    gate = _dot(h, wg)
    return gate * jax.nn.sigmoid(gate) * _dot(h, wu)


def _experts_kernel(tile_ref, exp_ref, lo_ref, hi_ref, xs_ref, wg_ref, wu_ref, wd_ref, y_ref, wgb, wub, wdb):
    v = pl.program_id(0)
    lo, hi = lo_ref[v], hi_ref[v]
    new_expert = jnp.logical_or(v == 0, exp_ref[v] != exp_ref[jnp.maximum(v - 1, 0)])

    @pl.when(new_expert)
    def _():
        wgb[...] = wg_ref[0].astype(BF16)
        wub[...] = wu_ref[0].astype(BF16)
        wdb[...] = wd_ref[0].astype(BF16)

    @pl.when(hi > lo)
    def _():
        y = _dot(_swiglu(xs_ref[...].astype(BF16), wgb[...], wub[...]).astype(BF16), wdb[...])

        @pl.when(lo == 0)
        def _():
            y_ref[...] = y

        @pl.when(lo > 0)
        def _():
            own = lax.broadcasted_iota(jnp.int32, (y.shape[0], 1), 0) >= lo
            y_ref[...] = jnp.where(own, y, y_ref[...])


def _experts(xs, visits, wg, wu, wd, tm):
    r, d = xs.shape
    n_exp, _, de = wg.shape
    n_visits = visits[0].shape[0]
    return pl.pallas_call(
        _experts_kernel,
        out_shape=jax.ShapeDtypeStruct((r, d), F32),
        grid_spec=pltpu.PrefetchScalarGridSpec(
            num_scalar_prefetch=4,
            grid=(n_visits,),
            in_specs=[pl.BlockSpec((tm, d), lambda v, t, e, lo, hi: (t[v], 0)),
                      pl.BlockSpec((1, d, de), lambda v, t, e, lo, hi: (e[v], 0, 0)),
                      pl.BlockSpec((1, d, de), lambda v, t, e, lo, hi: (e[v], 0, 0)),
                      pl.BlockSpec((1, de, d), lambda v, t, e, lo, hi: (e[v], 0, 0))],
            out_specs=pl.BlockSpec((tm, d), lambda v, t, e, lo, hi: (t[v], 0)),
            scratch_shapes=[pltpu.VMEM((d, de), BF16), pltpu.VMEM((d, de), BF16), pltpu.VMEM((de, d), BF16)],
        ),
        compiler_params=_params("arbitrary"),
        name="moe_experts",
    )(*visits, xs, wg, wu, wd)


def _visit_plan(counts, n_rows, tm):
    n_exp = counts.shape[0]
    ends = jnp.cumsum(counts)
    starts = ends - counts
    first_tile = starts // tm
    n_vis = jnp.where(counts > 0, (ends - 1) // tm - first_tile + 1, 0)
    vis_end = jnp.cumsum(n_vis)
    total = vis_end[-1]
    n_visits = n_rows // tm + n_exp - 1
    v = jnp.minimum(jnp.arange(n_visits, dtype=jnp.int32), total - 1)
    e = jnp.sum(v[:, None] >= vis_end[None, :], axis=1).astype(jnp.int32)
    of_e = functools.partial(_lookup, e)
    tile = (of_e(first_tile) + v - of_e(vis_end - n_vis)).astype(jnp.int32)
    lo = jnp.maximum(of_e(starts), tile * tm) - tile * tm
    hi = jnp.minimum(of_e(ends), (tile + 1) * tm) - tile * tm
    hi = jnp.where(jnp.arange(n_visits) < total, hi, lo)
    return (tile, e, lo.astype(jnp.int32), hi.astype(jnp.int32)), starts


def _lookup(idx, table):
    hit = idx[..., None] == jnp.arange(table.shape[0], dtype=idx.dtype)
    return jnp.sum(jnp.where(hit, table, 0), axis=-1)


def _combine_kernel(pos_ref, nxt_ref, w_ref, hn_ref, sg_ref, su_ref, sd_ref, y_ref, o_ref, buf, sem):
    i = pl.program_id(0)
    n_steps = pl.num_programs(0)
    tm = hn_ref.shape[0]

    def gather(p_ref, slot, wait):
        make_copy = lambda n, k: pltpu.make_async_copy(
            y_ref.at[pl.ds(p_ref[k, n], 1)], buf.at[slot, k, pl.ds(n, 1)], sem.at[slot])
        _row_copies(tm, make_copy, wait)

    @pl.when(i == 0)
    def _():
        gather(pos_ref, 0, wait=False)

    @pl.when(i + 1 < n_steps)
    def _():
        gather(nxt_ref, (i + 1) % 2, wait=False)

    acc = _dot(_swiglu(hn_ref[...].astype(BF16), sg_ref[...], su_ref[...]).astype(BF16), sd_ref[...])
    gather(pos_ref, i % 2, wait=True)
    w = w_ref[...]
    for k in range(TOP_K):
        acc = acc + w[:, k:k + 1] * buf[i % 2, k]
    o_ref[...] = acc


def _combine(y, pos, w_tok, hn, sg, su, sd, tm):
    n, d = hn.shape
    ds = sg.shape[1]
    n_steps = n // tm
    full = lambda a, b: pl.BlockSpec((a, b), lambda i: (0, 0))
    return pl.pallas_call(
        _combine_kernel,
        out_shape=jax.ShapeDtypeStruct((n, d), F32),
        grid=(n_steps,),
        in_specs=[pl.BlockSpec((TOP_K, tm), lambda i: (0, i), memory_space=pltpu.SMEM),
                  pl.BlockSpec((TOP_K, tm), lambda i: (0, jnp.minimum(i + 1, n_steps - 1)), memory_space=pltpu.SMEM),
                  pl.BlockSpec((tm, TOP_K), lambda i: (i, 0)),
                  pl.BlockSpec((tm, d), lambda i: (i, 0)),
                  full(d, ds), full(d, ds), full(ds, d),
                  pl.BlockSpec(memory_space=pl.ANY)],
        out_specs=pl.BlockSpec((tm, d), lambda i: (i, 0)),
        scratch_shapes=[pltpu.VMEM((2, TOP_K, tm, d), F32), pltpu.SemaphoreType.DMA((2,))],
        compiler_params=_params("arbitrary"),
        name="moe_combine",
    )(pos, pos, w_tok, hn, sg, su, sd, y)


def _ple_kernel(t_ref, m_ref, p_ref, gp_ref, wg_ref, wp_ref, gf_ref, o_ref):
    t = t_ref[...] + m_ref[...]
    gate = jax.nn.sigmoid(_dot(_rms(t, gp_ref[...]).astype(BF16), wg_ref[...]))
    pe = _dot(p_ref[...].astype(BF16), wp_ref[...])
    o_ref[...] = _rms(t + pe * gate, gf_ref[...])


def _ple(t, m, p, g_ple, w_gate_bf16, w_proj_bf16, g_final, tm):
    n, d = t.shape
    dp = p.shape[1]
    tile = lambda w: pl.BlockSpec((tm, w), lambda i: (i, 0))
    full = lambda a, b: pl.BlockSpec((a, b), lambda i: (0, 0))
    return pl.pallas_call(
        _ple_kernel,
        out_shape=jax.ShapeDtypeStruct((n, d), F32),
        grid=(n // tm,),
        in_specs=[tile(d), tile(d), tile(dp), full(1, d), full(d, d), full(dp, d), full(1, d)],
        out_specs=tile(d),
        compiler_params=_params("parallel"),
        name="ple_final",
    )(t, m, p, g_ple, w_gate_bf16, w_proj_bf16, g_final)


def _tile(n, candidates):
    for c in candidates:
        if n % c == 0:
            return c
    raise ValueError(f"no tile for {n}")


def kernel(x_prompt, x_sample, cache_k, cache_v, state_pool, page_table, p_prompt, p_sample, g_mix, w_in, sb_bias, w_pool, pool_scale, w_o, g_ffn, w_router, router_bias, w_exp_gate, w_exp_up, w_exp_down, w_sh_gate, w_sh_up, w_sh_down, g_ple, w_ple_gate, w_ple_proj, g_final):
    depth = w_in.shape[0]
    assert depth == 1 and x_prompt.shape[0] == 1
    _, seq, d = x_prompt.shape
    db, t_new, _ = x_sample.shape
    _, n_phys, page, heads, head_dim = cache_k.shape
    sbw = heads * head_dim
    past_len = page_table.shape[1] * page
    row = lambda a: a.reshape(1, -1).astype(F32)

    xp = x_prompt.reshape(seq, d)
    xs = x_sample.reshape(db * t_new, d)
    w_in_b = w_in[0].astype(BF16)

    qp, kp, vp, up, kpb, vpb = _inproj(xp, row(g_mix[0]), w_in_b, _tile(seq, (512, 256, 128)))
    qs, ks, vs, us, ksb, vsb = _inproj(xs, row(g_mix[0]), w_in_b, db * t_new)

    att_p = _sb_prompt(qp, kpb, vpb, sb_bias[0].astype(F32), head_dim, 256, 256, 4)
    att_s = _sb_sample(qs.reshape(db, t_new, sbw), ksb.reshape(db, t_new, sbw), vsb.reshape(db, t_new, sbw),
                       cache_k.reshape(n_phys, page * heads, head_dim), cache_v.reshape(n_phys, page * heads, head_dim),
                       page_table, sb_bias[0], heads, head_dim, _tile(page_table.shape[1], (8, 4, 2, 1)))

    w_pool_b = w_pool[0].astype(BF16)
    pool_p = _pool_prompt(up, w_pool_b, row(pool_scale[0]), 256)
    buf = jnp.concatenate([state_pool[0], us.reshape(db, t_new, -1)], axis=1)
    pool_s = _pool_sample(buf, w_pool_b, row(pool_scale[0]), t_new, past_len + t_new - 1)

    x_all = jnp.concatenate([xp, xs], axis=0)
    att_all = jnp.concatenate([att_p, att_s.reshape(db * t_new, sbw).astype(BF16)], axis=0)
    pool_all = jnp.concatenate([pool_p, pool_s], axis=0)
    n = x_all.shape[0]
    tm = _tile(n, (640, 512, 256, 128))
    t1 = _outproj(x_all, att_all, pool_all, w_o[0].astype(BF16), tm)

    hn, eid, rank, w_slot, counts = _router(t1, row(g_ffn[0]), w_router[0].T,
                                            router_bias[0].reshape(-1, 1).astype(F32), tm)
    row_tile = _tile(n * TOP_K, (256, 128))
    visits, starts = _visit_plan(counts[:, 0].astype(jnp.int32), n * TOP_K, row_tile)
    pos = _lookup(eid, starts) + rank
    xs_sorted = _dispatch(hn, pos, _tile(n, (128,)))
    y_sorted = _experts(xs_sorted, visits, w_exp_gate[0], w_exp_up[0], w_exp_down[0], row_tile)
    routed = _combine(y_sorted, pos, w_slot.T, hn, w_sh_gate[0].astype(BF16), w_sh_up[0].astype(BF16),
                      w_sh_down[0].astype(BF16), _tile(n, (128,)))

    p_all = jnp.concatenate([p_prompt[0].reshape(seq, -1), p_sample[0].reshape(db * t_new, -1)], axis=0)
    y = _ple(t1, routed, p_all, row(g_ple[0]), w_ple_gate[0].astype(BF16), w_ple_proj[0].astype(BF16),
             row(g_final), _tile(n, (320, 256, 128)))

    heads5 = lambda a, b, l: a.reshape(1, b, l, heads, head_dim)
    return (y[:seq].reshape(x_prompt.shape), y[seq:].reshape(x_sample.shape),
            heads5(kp, 1, seq), heads5(vp, 1, seq), up[None, None, seq - POOL_STATE:, :],
            heads5(ks, db, t_new), heads5(vs, db, t_new), buf[None, :, -POOL_STATE:, :])
```

```python
import functools

import jax
import jax.numpy as jnp
from jax import lax
from jax.experimental import pallas as pl
from jax.experimental.pallas import tpu as pltpu

F32 = jnp.float32
BF16 = jnp.bfloat16

RMS_EPS = 1e-6
POOL_WINDOWS = (2, 4, 8, 16)
POOL_STATE = max(POOL_WINDOWS) - 1
TOP_K = 8
N_EXPERT_GROUPS = 8
TOPK_GROUPS = 4
ROUTED_SCALE = 2.5

V7X_VMEM_LIMIT_BYTES = 56 * 1024 * 1024


def _params(*sem):
    return pltpu.CompilerParams(dimension_semantics=sem, vmem_limit_bytes=V7X_VMEM_LIMIT_BYTES)


def _rms(x, g):
    return x * lax.rsqrt(jnp.mean(x * x, axis=-1, keepdims=True) + RMS_EPS) * g


def _split3(x):
    hi = x.astype(BF16)
    r = x - hi.astype(F32)
    mid = r.astype(BF16)
    lo = (r - mid.astype(F32)).astype(BF16)
    return hi, mid, lo


def _dot(a, b):
    return jnp.dot(a, b, preferred_element_type=F32)


def _dot_nt(a, b):
    return lax.dot_general(a, b, (((1,), (1,)), ((), ())), preferred_element_type=F32)


def _inproj_kernel(x_ref, g_ref, w_ref, q_ref, k_ref, v_ref, u_ref, kb_ref, vb_ref, xn_ref, *, q_scale):
    j = pl.program_id(1)

    @pl.when(j == 0)
    def _():
        xn_ref[...] = _rms(x_ref[...], g_ref[...]).astype(BF16)

    z = _dot(xn_ref[...], w_ref[...])

    @pl.when(j == 0)
    def _():
        q_ref[...] = (z * q_scale).astype(BF16)

    @pl.when(j == 1)
    def _():
        k_ref[...] = z
        kb_ref[...] = z.astype(BF16)

    @pl.when(j == 2)
    def _():
        v_ref[...] = z
        vb_ref[...] = z.astype(BF16)

    @pl.when(j == 3)
    def _():
        u_ref[...] = z


def _inproj(x, g, w_bf16, tm, q_scale):
    n, d = x.shape
    wn = w_bf16.shape[1] // 4
    out = lambda dt: jax.ShapeDtypeStruct((n, wn), dt)
    ospec = pl.BlockSpec((tm, wn), lambda i, j: (i, 0))
    return pl.pallas_call(
        functools.partial(_inproj_kernel, q_scale=q_scale),
        out_shape=(out(BF16), out(F32), out(F32), out(F32), out(BF16), out(BF16)),
        grid=(n // tm, 4),
        in_specs=[pl.BlockSpec((tm, d), lambda i, j: (i, 0)),
                  pl.BlockSpec((1, d), lambda i, j: (0, 0)),
                  pl.BlockSpec((d, wn), lambda i, j: (0, j))],
        out_specs=(ospec,) * 6,
        scratch_shapes=[pltpu.VMEM((tm, d), BF16)],
        compiler_params=_params("parallel", "arbitrary"),
        name="inproj",
    )(x, g, w_bf16)


def _sb_logits(s, bias, mask):
    z = s + bias
    sp = jnp.maximum(z, 0.0) + jnp.log(1.0 + jnp.exp(-jnp.abs(z)))
    return z - sp, sp if mask is None else jnp.where(mask, sp, 0.0)


def _sb_after(sp, neg_upper):
    hi = sp.astype(BF16)
    lo = (sp - hi.astype(F32)).astype(BF16)
    return _dot(hi, neg_upper) + _dot(lo, neg_upper)


def _sb_weights(ls, sp, after, c, mask):
    a = jnp.exp(ls + after + c)
    if mask is not None:
        a = jnp.where(mask, a, 0.0)
    return a, c - jnp.sum(sp, axis=-1, keepdims=True)


def _sb_tile(s, bias, neg_upper, c, mask):
    ls, sp = _sb_logits(s, bias, mask)
    return _sb_weights(ls, sp, _sb_after(sp, neg_upper), c, mask)


def _strict_upper(t):
    later = lax.broadcasted_iota(jnp.int32, (t, t), 0) > lax.broadcasted_iota(jnp.int32, (t, t), 1)
    return jnp.where(later, -1.0, 0.0).astype(BF16)


def _sb_prompt_kernel(bias_ref, q_ref, k_ref, v_ref, o_ref, *, tq, tk, head_dim):
    hg = pl.program_id(0)
    i = pl.program_id(1)
    n_heads = q_ref.shape[1] // head_dim
    lanes = [slice(h * head_dim, (h + 1) * head_dim) for h in range(n_heads)]
    bias = [bias_ref[hg * n_heads + h] for h in range(n_heads)]
    q = [q_ref[:, l] for l in lanes]
    upper = _strict_upper(tk)
    last = (i * tq) // tk
    causal = (lax.broadcasted_iota(jnp.int32, (tq, tk), 1)
              < lax.broadcasted_iota(jnp.int32, (tq, tk), 0) + (i * tq - last * tk))

    def block(j, carry, mask):
        rows = pl.ds(pl.multiple_of(j * tk, tk), tk)
        hs = range(n_heads)
        s = [_dot_nt(q[h], k_ref[rows, lanes[h]]) for h in hs]
        lg = [_sb_logits(s[h], bias[h], mask) for h in hs]
        after = [_sb_after(lg[h][1], upper) for h in hs]
        w = [_sb_weights(lg[h][0], lg[h][1], after[h], carry[h][0], mask) for h in hs]
        return tuple((w[h][1], carry[h][1] + _dot(w[h][0].astype(BF16), v_ref[rows, lanes[h]])) for h in hs)

    zero = (jnp.zeros((tq, 1), F32), jnp.zeros((tq, head_dim), F32))
    carry = block(last, (zero,) * n_heads, causal)
    carry = lax.fori_loop(0, last, lambda jj, carry: block(last - 1 - jj, carry, None), carry)
    for h in range(n_heads):
        o_ref[:, lanes[h]] = carry[h][1].astype(o_ref.dtype)


def _sb_prompt(q, kb, vb, sb_bias, head_dim, tq, tk, heads_per_step):
    s, width = q.shape
    assert tk % tq == 0 and s % tk == 0
    gw = heads_per_step * head_dim
    kv_spec = pl.BlockSpec((s, gw), lambda h, i, b: (0, h))
    return pl.pallas_call(
        functools.partial(_sb_prompt_kernel, tq=tq, tk=tk, head_dim=head_dim),
        out_shape=jax.ShapeDtypeStruct((s, width), BF16),
        grid_spec=pltpu.PrefetchScalarGridSpec(
            num_scalar_prefetch=1,
            grid=(width // gw, s // tq),
            in_specs=[pl.BlockSpec((tq, gw), lambda h, i, b: (i, h)), kv_spec, kv_spec],
            out_specs=pl.BlockSpec((tq, gw), lambda h, i, b: (i, h)),
        ),
        compiler_params=_params("parallel", "arbitrary"),
        name="sb_prompt",
    )(sb_bias, q, kb, vb)


def _page_matrix(page_ref, heads):
    keys = page_ref.shape[1] // heads
    cols = [page_ref[0, pl.ds(h, keys, stride=heads), :] for h in range(heads)]
    return jnp.concatenate(cols, axis=1).astype(BF16)


def _sb_sample_kernel(pt_ref, qbd_ref, bias_ref, kn_ref, vn_ref, *refs, heads, head_dim, group):
    k_refs, v_refs = refs[:group], refs[group:2 * group]
    o_ref, c_ref, acc_ref = refs[2 * group:]
    p = pl.program_id(1)
    rows, page = qbd_ref.shape[1], kn_ref.shape[1]
    qbd = qbd_ref[0]
    bias = bias_ref[...]
    upper = _strict_upper(page)

    @pl.when(p == 0)
    def _():
        t_of_row = lax.broadcasted_iota(jnp.int32, (rows, page), 0) // heads
        mask = lax.broadcasted_iota(jnp.int32, (rows, page), 1) < t_of_row
        a, c = _sb_tile(_dot_nt(qbd, kn_ref[0]), bias, upper, jnp.zeros((rows, 1), F32), mask)
        c_ref[...] = c
        acc_ref[...] = _dot(a.astype(BF16), vn_ref[0])

    gs = range(group)
    s = [_dot_nt(qbd, _page_matrix(k_refs[g], heads)) for g in gs]
    lg = [_sb_logits(s[g], bias, None) for g in gs]
    after = [_sb_after(lg[g][1], upper) for g in gs]
    c = c_ref[...]
    acc = acc_ref[...]
    for g in gs:
        a, c = _sb_weights(lg[g][0], lg[g][1], after[g], c, None)
        acc = acc + _dot(a.astype(BF16), _page_matrix(v_refs[g], heads))
    c_ref[...] = c
    acc_ref[...] = acc

    @pl.when(p == pl.num_programs(1) - 1)
    def _():
        width = heads * head_dim
        lane_head = lax.broadcasted_iota(jnp.int32, (heads, width), 1) // head_dim
        own = lane_head == lax.broadcasted_iota(jnp.int32, (heads, width), 0)
        acc = acc_ref[...].reshape(rows // heads, heads, width)
        o_ref[0] = jnp.sum(jnp.where(own[None], acc, 0.0), axis=1)


def _sb_sample(q, k_new, v_new, cache_k, cache_v, page_table, sb_bias, heads, head_dim, group):
    db, t, width = q.shape
    n_pages = page_table.shape[1]
    page = cache_k.shape[1] // heads
    assert n_pages % group == 0
    rows = t * heads
    lane_head = jnp.arange(width) // head_dim
    own = (lane_head[None, :] == jnp.arange(heads)[:, None])
    qbd = jnp.where(own[None, None], q[:, :, None, :], jnp.zeros((), q.dtype)).reshape(db, rows, width)
    bias_rows = jnp.tile(sb_bias.astype(F32), t).reshape(rows, 1)
    pad = lambda a: jnp.pad(a, ((0, 0), (0, page - t), (0, 0)))
    new_spec = pl.BlockSpec((1, page, width), lambda b, p, pt: (b, 0, 0))
    page_specs = [pl.BlockSpec((1, page * heads, head_dim),
                               lambda b, p, pt, g=g: (pt[b, n_pages - 1 - (p * group + g)], 0, 0))
                  for g in range(group)]
    return pl.pallas_call(
        functools.partial(_sb_sample_kernel, heads=heads, head_dim=head_dim, group=group),
        out_shape=jax.ShapeDtypeStruct((db, t, width), F32),
        grid_spec=pltpu.PrefetchScalarGridSpec(
            num_scalar_prefetch=1,
            grid=(db, n_pages // group),
            in_specs=[pl.BlockSpec((1, rows, width), lambda b, p, pt: (b, 0, 0)),
                      pl.BlockSpec((rows, 1), lambda b, p, pt: (0, 0)),
                      new_spec, new_spec] + page_specs + page_specs,
            out_specs=pl.BlockSpec((1, t, width), lambda b, p, pt: (b, 0, 0)),
            scratch_shapes=[pltpu.VMEM((rows, 1), F32), pltpu.VMEM((rows, width), F32)],
        ),
        compiler_params=_params("parallel", "arbitrary"),
        name="sb_sample",
    )(page_table, qbd, bias_rows, pad(k_new), pad(v_new), *([cache_k] * group), *([cache_v] * group))


def _pool_groups(ext, tok, band_fn, inv_cnt_fn, wp_ref, scale_ref, o_ref):
    group = wp_ref.shape[1]
    for gi, w in enumerate(POOL_WINDOWS):
        cols = slice(gi * group, (gi + 1) * group)
        band = band_fn(w)
        hi, mid, lo = _split3(ext[:, cols])
        wsum = _dot(band, hi) + _dot(band, mid) + _dot(band, lo)
        d = wsum * inv_cnt_fn(w) - tok[:, cols]
        o_ref[:, cols] = (_dot(d.astype(BF16), wp_ref[gi]) * scale_ref[:, cols]).astype(o_ref.dtype)


def _pool_prompt_kernel(u_ref, prev_ref, wp_ref, scale_ref, o_ref, *, tm, halo):
    i = pl.program_id(0)
    tok = u_ref[...]
    prev = jnp.where(i > 0, prev_ref[...], 0.0)
    ext = jnp.concatenate([prev, tok], axis=0)
    r = lax.broadcasted_iota(jnp.int32, (tm, tm + halo), 0)
    cc = lax.broadcasted_iota(jnp.int32, (tm, tm + halo), 1) - halo
    pos = i * tm + lax.broadcasted_iota(jnp.int32, (tm, 1), 0)

    band_fn = lambda w: ((cc <= r) & (cc > r - w)).astype(BF16)
    inv_cnt_fn = lambda w: 1.0 / jnp.minimum(pos + 1, w).astype(F32)
    _pool_groups(ext, tok, band_fn, inv_cnt_fn, wp_ref, scale_ref, o_ref)


def _pool_prompt(u, w_pool_bf16, pool_scale, tm):
    s, width = u.shape
    halo = 128
    assert halo > POOL_STATE and tm % halo == 0
    return pl.pallas_call(
        functools.partial(_pool_prompt_kernel, tm=tm, halo=halo),
        out_shape=jax.ShapeDtypeStruct((s, width), BF16),
        grid=(s // tm,),
        in_specs=[pl.BlockSpec((tm, width), lambda i: (i, 0)),
                  pl.BlockSpec((halo, width), lambda i: (jnp.maximum(i * (tm // halo) - 1, 0), 0)),
                  pl.BlockSpec(w_pool_bf16.shape, lambda i: (0, 0, 0)),
                  pl.BlockSpec((1, width), lambda i: (0, 0))],
        out_specs=pl.BlockSpec((tm, width), lambda i: (i, 0)),
        compiler_params=_params("parallel"),
        name="pool_prompt",
    )(u, u, w_pool_bf16, pool_scale)


def _pool_sample_kernel(buf_ref, wp_ref, scale_ref, o_ref, *, t_new, buf_len, last_pos):
    ext = buf_ref[...]
    n_out = o_ref.shape[0]
    r = lax.broadcasted_iota(jnp.int32, (n_out, ext.shape[0]), 0)
    cc = lax.broadcasted_iota(jnp.int32, (n_out, ext.shape[0]), 1)
    own_row = (r // t_new) * buf_len + (buf_len - t_new) + r % t_new
    pick = (cc == own_row).astype(BF16)
    tok = sum(_dot(pick, part) for part in _split3(ext))
    pos = last_pos - (t_new - 1) + lax.broadcasted_iota(jnp.int32, (n_out, 1), 0) % t_new

    band_fn = lambda w: ((cc <= own_row) & (cc > own_row - w)).astype(BF16)
    inv_cnt_fn = lambda w: 1.0 / jnp.minimum(pos + 1, w).astype(F32)
    _pool_groups(ext, tok, band_fn, inv_cnt_fn, wp_ref, scale_ref, o_ref)


def _pool_sample(buf, w_pool_bf16, pool_scale, t_new, last_pos):
    db, buf_len, width = buf.shape
    lead = max(POOL_WINDOWS) - (buf_len - t_new) - 1
    assert lead >= 0
    while (db * (buf_len + lead)) % 128:
        lead += 1
    padded = jnp.pad(buf, ((0, 0), (lead, 0), (0, 0)))
    full = lambda shape: pl.BlockSpec(shape, lambda i: (0,) * len(shape))
    return pl.pallas_call(
        functools.partial(_pool_sample_kernel, t_new=t_new, buf_len=buf_len + lead, last_pos=last_pos),
        out_shape=jax.ShapeDtypeStruct((db * t_new, width), BF16),
        grid=(1,),
        in_specs=[full((db * (buf_len + lead), width)), full(w_pool_bf16.shape), full((1, width))],
        out_specs=full((db * t_new, width)),
        compiler_params=_params("arbitrary"),
        name="pool_sample",
    )(padded.reshape(db * (buf_len + lead), width), w_pool_bf16, pool_scale)


def _outproj_kernel(x_ref, att_ref, pool_ref, wa_ref, wp_ref, o_ref):
    o_ref[...] = x_ref[...] + _dot(att_ref[...], wa_ref[...]) + _dot(pool_ref[...], wp_ref[...])


def _outproj(x, att, pool, w_o_bf16, tm):
    n, d = x.shape
    half = att.shape[1]
    return pl.pallas_call(
        _outproj_kernel,
        out_shape=jax.ShapeDtypeStruct((n, d), F32),
        grid=(n // tm,),
        in_specs=[pl.BlockSpec((tm, d), lambda i: (i, 0)),
                  pl.BlockSpec((tm, half), lambda i: (i, 0)),
                  pl.BlockSpec((tm, half), lambda i: (i, 0)),
                  pl.BlockSpec((half, d), lambda i: (0, 0)),
                  pl.BlockSpec((half, d), lambda i: (1, 0))],
        out_specs=pl.BlockSpec((tm, d), lambda i: (i, 0)),
        compiler_params=_params("parallel"),
        name="outproj",
    )(x, att, pool, w_o_bf16, w_o_bf16)


def _rank_lt(x, k):
    nrows = x.shape[0]
    ridx = lax.broadcasted_iota(jnp.int32, x.shape, 0)
    rank = jnp.zeros(x.shape, jnp.int32)
    for e in range(nrows):
        row = x[e:e + 1, :]
        ahead = (row > x) | ((row == x) & (e < ridx))
        rank = rank + ahead.astype(jnp.int32)
    return rank < k


def _router_kernel(t_ref, g_ref, wr_ref, rb_ref, hn_ref, eid_ref, rank_ref, w_ref, cnt_ref):
    @pl.when(pl.program_id(0) == 0)
    def _():
        cnt_ref[...] = jnp.zeros(cnt_ref.shape, F32)

    hn = _rms(t_ref[...], g_ref[...])
    hn_ref[...] = hn
    h_hi, h_mid, _ = _split3(hn)
    w_hi, w_mid, _ = _split3(wr_ref[...])
    logits = _dot_nt(w_hi, h_hi) + _dot_nt(w_hi, h_mid) + _dot_nt(w_mid, h_hi)
    s = jax.nn.sigmoid(logits)
    sb = s + rb_ref[...]
    n_exp, tm = sb.shape
    per_group = n_exp // N_EXPERT_GROUPS
    g3 = sb.reshape(N_EXPERT_GROUPS, per_group, tm)
    m1 = jnp.max(g3, axis=1, keepdims=True)
    is_top = g3 == m1
    n_top = jnp.sum(is_top.astype(jnp.int32), axis=1, keepdims=True)
    m2 = jnp.where(n_top > 1, m1, jnp.max(jnp.where(is_top, -jnp.inf, g3), axis=1, keepdims=True))
    gscore = (m1 + m2).reshape(N_EXPERT_GROUPS, tm)
    gsel = _rank_lt(gscore, TOPK_GROUPS)
    emask = jnp.broadcast_to(gsel[:, None, :], g3.shape).reshape(n_exp, tm)
    sel = _rank_lt(jnp.where(emask, sb, -jnp.inf), TOP_K) & emask
    wsel = jnp.where(sel, s, 0.0)
    comb = wsel / jnp.sum(wsel, axis=0, keepdims=True) * ROUTED_SCALE

    self = jnp.where(sel, 1.0, 0.0)
    selb = self.astype(BF16)
    lower = (lax.broadcasted_iota(jnp.int32, (n_exp, n_exp), 0)
             > lax.broadcasted_iota(jnp.int32, (n_exp, n_exp), 1)).astype(BF16)
    earlier = (lax.broadcasted_iota(jnp.int32, (tm, tm), 0)
               < lax.broadcasted_iota(jnp.int32, (tm, tm), 1)).astype(BF16)
    slot = _dot(lower, selb)
    rank = _dot(selb, earlier) + cnt_ref[:, 0:1]
    eidx = lax.broadcasted_iota(jnp.int32, (n_exp, tm), 0).astype(F32)
    pick = lambda m, v: jnp.sum(jnp.where(m, v, 0.0), axis=0, keepdims=True)
    eids, ranks, ws = [], [], []
    for k in range(TOP_K):
        m = sel & (slot == float(k))
        eids.append(pick(m, eidx))
        ranks.append(pick(m, rank))
        ws.append(pick(m, comb))
    eid_ref[...] = jnp.concatenate(eids, axis=0).astype(jnp.int32)
    rank_ref[...] = jnp.concatenate(ranks, axis=0).astype(jnp.int32)
    w_ref[...] = jnp.concatenate(ws, axis=0)
    cnt_ref[...] += jnp.sum(self, axis=1, keepdims=True)


def _router(t, g, w_router_t, router_bias, tm):
    n, d = t.shape
    n_exp = w_router_t.shape[0]
    slots = lambda dt: jax.ShapeDtypeStruct((TOP_K, n), dt)
    slot_spec = pl.BlockSpec((TOP_K, tm), lambda i: (0, i))
    return pl.pallas_call(
        _router_kernel,
        out_shape=(jax.ShapeDtypeStruct((n, d), F32), slots(jnp.int32), slots(jnp.int32), slots(F32),
                   jax.ShapeDtypeStruct((n_exp, 128), F32)),
        grid=(n // tm,),
        in_specs=[pl.BlockSpec((tm, d), lambda i: (i, 0)),
                  pl.BlockSpec((1, d), lambda i: (0, 0)),
                  pl.BlockSpec((n_exp, d), lambda i: (0, 0)),
                  pl.BlockSpec((n_exp, 1), lambda i: (0, 0))],
        out_specs=(pl.BlockSpec((tm, d), lambda i: (i, 0)), slot_spec, slot_spec, slot_spec,
                   pl.BlockSpec((n_exp, 128), lambda i: (0, 0))),
        compiler_params=_params("arbitrary"),
        name="router",
    )(t, g, w_router_t, router_bias)


def _row_copies(n_rows, make_copy, wait):
    def body(n, carry):
        for k in range(TOP_K):
            cp = make_copy(n, k)
            cp.wait() if wait else cp.start(priority=k % 2)
        return carry
    lax.fori_loop(0, n_rows, body, 0)


def _dispatch_kernel(pos_ref, hn_ref, xs_ref, sem):
    make_copy = lambda n, k: pltpu.make_async_copy(
        hn_ref.at[pl.ds(n, 1)], xs_ref.at[pl.ds(pos_ref[k, n], 1)], sem)
    _row_copies(hn_ref.shape[0], make_copy, wait=False)
    _row_copies(hn_ref.shape[0], make_copy, wait=True)


def _dispatch(hn, pos, tm):
    n, d = hn.shape
    return pl.pallas_call(
        _dispatch_kernel,
        out_shape=jax.ShapeDtypeStruct((n * TOP_K, d), F32),
        grid=(n // tm,),
        in_specs=[pl.BlockSpec((TOP_K, tm), lambda i: (0, i), memory_space=pltpu.SMEM),
                  pl.BlockSpec((tm, d), lambda i: (i, 0))],
        out_specs=pl.BlockSpec(memory_space=pl.ANY),
        scratch_shapes=[pltpu.SemaphoreType.DMA(())],
        compiler_params=_params("arbitrary"),
        name="moe_dispatch",
    )(pos, hn)


def _swiglu(h, wg, wu):
    gate = _dot(h, wg)
    return gate * jax.nn.sigmoid(gate) * _dot(h, wu)


def _experts_kernel(tile_ref, exp_ref, lo_ref, hi_ref, xs_ref, wg_ref, wu_ref, wd_ref, y_ref, wgb, wub, wdb):
    v = pl.program_id(0)
    lo, hi = lo_ref[v], hi_ref[v]
    new_expert = jnp.logical_or(v == 0, exp_ref[v] != exp_ref[jnp.maximum(v - 1, 0)])

    @pl.when(new_expert)
    def _():
        wgb[...] = wg_ref[0].astype(BF16)
        wub[...] = wu_ref[0].astype(BF16)
        wdb[...] = wd_ref[0].astype(BF16)

    @pl.when(hi > lo)
    def _():
        y = _dot(_swiglu(xs_ref[...].astype(BF16), wgb[...], wub[...]).astype(BF16), wdb[...])

        @pl.when(lo == 0)
        def _():
            y_ref[...] = y

        @pl.when(lo > 0)
        def _():
            own = lax.broadcasted_iota(jnp.int32, (y.shape[0], 1), 0) >= lo
            y_ref[...] = jnp.where(own, y, y_ref[...])


def _experts(xs, visits, wg, wu, wd, tm):
    r, d = xs.shape
    n_exp, _, de = wg.shape
    n_visits = visits[0].shape[0]
    return pl.pallas_call(
        _experts_kernel,
        out_shape=jax.ShapeDtypeStruct((r, d), F32),
        grid_spec=pltpu.PrefetchScalarGridSpec(
            num_scalar_prefetch=4,
            grid=(n_visits,),
            in_specs=[pl.BlockSpec((tm, d), lambda v, t, e, lo, hi: (t[v], 0)),
                      pl.BlockSpec((1, d, de), lambda v, t, e, lo, hi: (e[v], 0, 0)),
                      pl.BlockSpec((1, d, de), lambda v, t, e, lo, hi: (e[v], 0, 0)),
                      pl.BlockSpec((1, de, d), lambda v, t, e, lo, hi: (e[v], 0, 0))],
            out_specs=pl.BlockSpec((tm, d), lambda v, t, e, lo, hi: (t[v], 0)),
            scratch_shapes=[pltpu.VMEM((d, de), BF16), pltpu.VMEM((d, de), BF16), pltpu.VMEM((de, d), BF16)],
        ),
        compiler_params=_params("arbitrary"),
        name="moe_experts",
    )(*visits, xs, wg, wu, wd)


def _visit_plan(counts, n_rows, tm):
    n_exp = counts.shape[0]
    ends = jnp.cumsum(counts)
    starts = ends - counts
    first_tile = starts // tm
    n_vis = jnp.where(counts > 0, (ends - 1) // tm - first_tile + 1, 0)
    vis_end = jnp.cumsum(n_vis)
    total = vis_end[-1]
    n_visits = n_rows // tm + n_exp - 1
    v = jnp.minimum(jnp.arange(n_visits, dtype=jnp.int32), total - 1)
    e = jnp.sum(v[:, None] >= vis_end[None, :], axis=1).astype(jnp.int32)
    of_e = functools.partial(_lookup, e)
    tile = (of_e(first_tile) + v - of_e(vis_end - n_vis)).astype(jnp.int32)
    lo = jnp.maximum(of_e(starts), tile * tm) - tile * tm
    hi = jnp.minimum(of_e(ends), (tile + 1) * tm) - tile * tm
    hi = jnp.where(jnp.arange(n_visits) < total, hi, lo)
    return (tile, e, lo.astype(jnp.int32), hi.astype(jnp.int32)), starts


def _lookup(idx, table):
    hit = idx[..., None] == jnp.arange(table.shape[0], dtype=idx.dtype)
    return jnp.sum(jnp.where(hit, table, 0), axis=-1)


def _combine_kernel(pos_ref, nxt_ref, w_ref, hn_ref, sg_ref, su_ref, sd_ref, y_ref, o_ref, buf, sem):
    i = pl.program_id(0)
    n_steps = pl.num_programs(0)
    tm = hn_ref.shape[0]

    def gather(p_ref, slot, wait):
        make_copy = lambda n, k: pltpu.make_async_copy(
            y_ref.at[pl.ds(p_ref[k, n], 1)], buf.at[slot, k, pl.ds(n, 1)], sem.at[slot])
        _row_copies(tm, make_copy, wait)

    @pl.when(i == 0)
    def _():
        gather(pos_ref, 0, wait=False)

    @pl.when(i + 1 < n_steps)
    def _():
        gather(nxt_ref, (i + 1) % 2, wait=False)

    acc = _dot(_swiglu(hn_ref[...].astype(BF16), sg_ref[...], su_ref[...]).astype(BF16), sd_ref[...])
    gather(pos_ref, i % 2, wait=True)
    w = w_ref[...]
    for k in range(TOP_K):
        acc = acc + w[:, k:k + 1] * buf[i % 2, k]
    o_ref[...] = acc


def _combine(y, pos, w_tok, hn, sg, su, sd, tm):
    n, d = hn.shape
    ds = sg.shape[1]
    n_steps = n // tm
    full = lambda a, b: pl.BlockSpec((a, b), lambda i: (0, 0))
    return pl.pallas_call(
        _combine_kernel,
        out_shape=jax.ShapeDtypeStruct((n, d), F32),
        grid=(n_steps,),
        in_specs=[pl.BlockSpec((TOP_K, tm), lambda i: (0, i), memory_space=pltpu.SMEM),
                  pl.BlockSpec((TOP_K, tm), lambda i: (0, jnp.minimum(i + 1, n_steps - 1)), memory_space=pltpu.SMEM),
                  pl.BlockSpec((tm, TOP_K), lambda i: (i, 0)),
                  pl.BlockSpec((tm, d), lambda i: (i, 0)),
                  full(d, ds), full(d, ds), full(ds, d),
                  pl.BlockSpec(memory_space=pl.ANY)],
        out_specs=pl.BlockSpec((tm, d), lambda i: (i, 0)),
        scratch_shapes=[pltpu.VMEM((2, TOP_K, tm, d), F32), pltpu.SemaphoreType.DMA((2,))],
        compiler_params=_params("arbitrary"),
        name="moe_combine",
    )(pos, pos, w_tok, hn, sg, su, sd, y)


def _ple_kernel(t_ref, m_ref, p_ref, gp_ref, wg_ref, wp_ref, gf_ref, o_ref):
    t = t_ref[...] + m_ref[...]
    gate = jax.nn.sigmoid(_dot(_rms(t, gp_ref[...]).astype(BF16), wg_ref[...]))
    pe = _dot(p_ref[...].astype(BF16), wp_ref[...])
    o_ref[...] = _rms(t + pe * gate, gf_ref[...])


def _ple(t, m, p, g_ple, w_gate_bf16, w_proj_bf16, g_final, tm, first_row):
    n, dp = p.shape
    d = t.shape[1]
    assert first_row % tm == 0 and n % tm == 0
    tile = lambda w: pl.BlockSpec((tm, w), lambda i: (i, 0))
    shifted = pl.BlockSpec((tm, d), lambda i: (i + first_row // tm, 0))
    full = lambda a, b: pl.BlockSpec((a, b), lambda i: (0, 0))
    return pl.pallas_call(
        _ple_kernel,
        out_shape=jax.ShapeDtypeStruct((n, d), F32),
        grid=(n // tm,),
        in_specs=[shifted, shifted, tile(dp), full(1, d), full(d, d), full(dp, d), full(1, d)],
        out_specs=tile(d),
        compiler_params=_params("parallel"),
        name="ple_final",
    )(t, m, p, g_ple, w_gate_bf16, w_proj_bf16, g_final)


def _tile(n, candidates):
    for c in candidates:
        if n % c == 0:
            return c
    raise ValueError(f"no tile for {n}")


def kernel(x_prompt, x_sample, cache_k, cache_v, state_pool, page_table, p_prompt, p_sample, g_mix, w_in, sb_bias, w_pool, pool_scale, w_o, g_ffn, w_router, router_bias, w_exp_gate, w_exp_up, w_exp_down, w_sh_gate, w_sh_up, w_sh_down, g_ple, w_ple_gate, w_ple_proj, g_final):
    depth = w_in.shape[0]
    assert depth == 1 and x_prompt.shape[0] == 1
    _, seq, d = x_prompt.shape
    db, t_new, _ = x_sample.shape
    _, n_phys, page, heads, head_dim = cache_k.shape
    sbw = heads * head_dim
    past_len = page_table.shape[1] * page
    row = lambda a: a.reshape(1, -1).astype(F32)

    xp = x_prompt.reshape(seq, d)
    xs = x_sample.reshape(db * t_new, d)
    w_in_b = w_in[0].astype(BF16)

    qp, kp, vp, up, kpb, vpb = _inproj(xp, row(g_mix[0]), w_in_b, _tile(seq, (512, 256, 128)), head_dim ** -0.5)
    qs, ks, vs, us, ksb, vsb = _inproj(xs, row(g_mix[0]), w_in_b, db * t_new, head_dim ** -0.5)

    att_p = _sb_prompt(qp, kpb, vpb, sb_bias[0].astype(F32), head_dim, 256, 256, 4)
    att_s = _sb_sample(qs.reshape(db, t_new, sbw), ksb.reshape(db, t_new, sbw), vsb.reshape(db, t_new, sbw),
                       cache_k.reshape(n_phys, page * heads, head_dim), cache_v.reshape(n_phys, page * heads, head_dim),
                       page_table, sb_bias[0], heads, head_dim, _tile(page_table.shape[1], (16, 8, 4, 2, 1)))

    w_pool_b = w_pool[0].astype(BF16)
    pool_p = _pool_prompt(up, w_pool_b, row(pool_scale[0]), 256)
    buf = jnp.concatenate([state_pool[0], us.reshape(db, t_new, -1)], axis=1)
    pool_s = _pool_sample(buf, w_pool_b, row(pool_scale[0]), t_new, past_len + t_new - 1)

    x_all = jnp.concatenate([xp, xs], axis=0)
    att_all = jnp.concatenate([att_p, att_s.reshape(db * t_new, sbw).astype(BF16)], axis=0)
    pool_all = jnp.concatenate([pool_p, pool_s], axis=0)
    n = x_all.shape[0]
    tm = _tile(n, (640, 512, 256, 128))
    t1 = _outproj(x_all, att_all, pool_all, w_o[0].astype(BF16), tm)

    hn, eid, rank, w_slot, counts = _router(t1, row(g_ffn[0]), w_router[0].T,
                                            router_bias[0].reshape(-1, 1).astype(F32), tm)
    row_tile = _tile(n * TOP_K, (256, 128))
    visits, starts = _visit_plan(counts[:, 0].astype(jnp.int32), n * TOP_K, row_tile)
    pos = _lookup(eid, starts) + rank
    xs_sorted = _dispatch(hn, pos, _tile(n, (128,)))
    y_sorted = _experts(xs_sorted, visits, w_exp_gate[0], w_exp_up[0], w_exp_down[0], row_tile)
    routed = _combine(y_sorted, pos, w_slot.T, hn, w_sh_gate[0].astype(BF16), w_sh_up[0].astype(BF16),
                      w_sh_down[0].astype(BF16), _tile(n, (128,)))

    ple = functools.partial(_ple, t1, routed, g_ple=row(g_ple[0]), w_gate_bf16=w_ple_gate[0].astype(BF16),
                            w_proj_bf16=w_ple_proj[0].astype(BF16), g_final=row(g_final))
    y_p = ple(p_prompt[0].reshape(seq, -1), tm=_tile(seq, (256, 128)), first_row=0)
    y_s = ple(p_sample[0].reshape(db * t_new, -1), tm=_tile(seq, (db * t_new,)), first_row=seq)

    heads5 = lambda a, b, l: a.reshape(1, b, l, heads, head_dim)
    return (y_p.reshape(x_prompt.shape), y_s.reshape(x_sample.shape),
            heads5(kp, 1, seq), heads5(vp, 1, seq), up[None, None, seq - POOL_STATE:, :],
            heads5(ks, db, t_new), heads5(vs, db, t_new), buf[None, :, -POOL_STATE:, :])
```
